```python
import math
import jax, jax.numpy as jnp
from jax import lax
import numpy as np

D_MODEL = 1024
BATCH = 16
SEQ = 2048
DEPTH = 4

N_A_LAYERS = DEPTH // 2
N_B_LAYERS = DEPTH - N_A_LAYERS
CONV_KERNEL = 31
HEAD_DIM = 64
N_HEADS = D_MODEL // HEAD_DIM
DILATED_GROUPS = ((128, 1), (512, 4), (2048, 16))
N_DGROUPS = len(DILATED_GROUPS)
GROUP_WIDTH = N_HEADS * HEAD_DIM
BLOCK = 128
ROPE_THETA = 10000.0
N_EXPERT_GROUPS = 4
EXPERTS_PER_GROUP = 4
N_EXPERTS = N_EXPERT_GROUPS * EXPERTS_PER_GROUP
TOP_K_INNER = 2
D_EXPERT = D_MODEL // 2
EPS = 1e-6

kernel_name = "yoco_conformer_dilated_attn_hmoe"


def rms_norm(x, g):
    xf = x.astype(jnp.float32)
    y = xf * lax.rsqrt(jnp.mean(xf * xf, axis=-1, keepdims=True) + EPS)
    return (y * g.astype(jnp.float32)).astype(x.dtype)


def layer_norm(x, g, b):
    xf = x.astype(jnp.float32)
    mu = jnp.mean(xf, axis=-1, keepdims=True)
    var = jnp.mean(jnp.square(xf - mu), axis=-1, keepdims=True)
    y = (xf - mu) * lax.rsqrt(var + EPS)
    return (y * g.astype(jnp.float32) + b.astype(jnp.float32)).astype(x.dtype)


def rotary(x):
    S = x.shape[1]
    pos = jnp.arange(S, dtype=jnp.float32)
    inv_freq = ROPE_THETA ** (-jnp.arange(0, HEAD_DIM, 2, dtype=jnp.float32) / HEAD_DIM)
    ang = pos[:, None] * inv_freq[None, :]
    bshape = (1, S) + (1,) * (x.ndim - 3) + (HEAD_DIM // 2,)
    cos = jnp.cos(ang).reshape(bshape).astype(x.dtype)
    sin = jnp.sin(ang).reshape(bshape).astype(x.dtype)
    x1, x2 = jnp.split(x, 2, axis=-1)
    return jnp.concatenate([x1 * cos - x2 * sin, x2 * cos + x1 * sin], axis=-1)


def conformer_conv(h, w_in, b_in, w_dw, b_dw, ln_g, ln_b, w_out, b_out):
    u = h @ w_in + b_in
    a, gate = jnp.split(u, 2, axis=-1)
    u = a * jax.nn.sigmoid(gate)
    u = lax.conv_general_dilated(
        u, w_dw[:, None, :], window_strides=(1,), padding=[(CONV_KERNEL - 1, 0)],
        dimension_numbers=("NWC", "WIO", "NWC"), feature_group_count=D_MODEL) + b_dw
    u = jax.nn.silu(layer_norm(u, ln_g, ln_b))
    return u @ w_out + b_out


def to_phase_blocks(x, dil):
    B, S = x.shape[:2]
    rest = x.shape[2:]
    L = S // dil
    x = jnp.swapaxes(x.reshape((B, L, dil) + rest), 1, 2)
    nb = -(-L // BLOCK)
    x = jnp.pad(x, [(0, 0), (0, 0), (0, nb * BLOCK - L)] + [(0, 0)] * len(rest))
    return x.reshape((B, dil, nb, BLOCK) + rest)


def from_phase_blocks(x, S, dil):
    B = x.shape[0]
    rest = x.shape[4:]
    L = S // dil
    x = x.reshape((B, dil, -1) + rest)[:, :, :L]
    return jnp.swapaxes(x, 1, 2).reshape((B, S) + rest)


def key_blocks(k, dil):
    kb = to_phase_blocks(k, dil)
    prev = jnp.pad(kb[:, :, :-1], [(0, 0), (0, 0), (1, 0), (0, 0), (0, 0), (0, 0)])
    return jnp.concatenate([prev, kb], axis=3)


def shared_kv(x, kv_norm, w_kv):
    h = rms_norm(x, kv_norm)
    B, S, _ = h.shape
    kv = (h @ w_kv).reshape(B, S, 2, N_DGROUPS, N_HEADS, HEAD_DIM)
    k = rotary(kv[:, :, 0])
    v = kv[:, :, 1]
    return [(key_blocks(k[:, :, g], dil), key_blocks(v[:, :, g], dil))
            for g, (_, dil) in enumerate(DILATED_GROUPS)]


def dilated_group_attention(q, k_cat, v_cat, dil, w_phase):
    S = q.shape[1]
    qb = to_phase_blocks(q, dil)
    nb = qb.shape[2]
    s = jnp.einsum("bpnihd,bpnjhd->bpnhij", qb.astype(jnp.float32), k_cat.astype(jnp.float32))
    i = jnp.arange(BLOCK)[:, None]
    j = jnp.arange(2 * BLOCK)[None, :]
    delta = BLOCK + i - j
    band = (delta >= 0) & (delta <= w_phase)
    blk = jnp.arange(nb)[:, None, None]
    valid = band[None] & ((blk > 0) | (j >= BLOCK)[None])
    s = jnp.where(valid[None, None, :, None], s, -jnp.inf)
    lse = jax.nn.logsumexp(s, axis=-1)
    p = jnp.exp(s - lse[..., None])
    o = jnp.einsum("bpnhij,bpnjhd->bpnihd", p.astype(v_cat.dtype), v_cat)
    return from_phase_blocks(o, S, dil), from_phase_blocks(jnp.moveaxis(lse, -1, 3), S, dil)


def dilated_attention(h, w_q, w_o, kv_blocks):
    B, S, _ = h.shape
    q = rotary((h @ w_q).reshape(B, S, N_DGROUPS, N_HEADS, HEAD_DIM)) * (HEAD_DIM ** -0.5)
    outs, lses = [], []
    for g, (window, dil) in enumerate(DILATED_GROUPS):
        k_cat, v_cat = kv_blocks[g]
        o, lse = dilated_group_attention(q[:, :, g], k_cat, v_cat, dil, window // dil)
        outs.append(o)
        lses.append(lse)
    o = jnp.stack(outs, axis=0).astype(jnp.float32)
    wts = jax.nn.softmax(jnp.stack(lses, axis=0), axis=0)
    comb = jnp.sum(wts[..., None] * o, axis=0).astype(h.dtype)
    return comb.reshape(B, S, GROUP_WIDTH) @ w_o


def hierarchical_moe(h, r_grp, r_exp, w_gate, w_up, w_down):
    B, S, D = h.shape
    t = h.reshape(B * S, D)
    p_grp = jax.nn.softmax((t @ r_grp).astype(jnp.float32), axis=-1)
    g_idx = jnp.argmax(p_grp, axis=-1)
    p_top = jnp.take_along_axis(p_grp, g_idx[:, None], axis=1)[:, 0]
    exp_logits = jnp.einsum("td,gde->tge", t, r_exp).astype(jnp.float32)
    sel = jnp.take_along_axis(exp_logits, g_idx[:, None, None], axis=1)[:, 0]
    top_v, top_i = lax.top_k(sel, TOP_K_INNER)
    w_in = jax.nn.softmax(top_v, axis=-1) * p_top[:, None]
    e_idx = g_idx[:, None] * EXPERTS_PER_GROUP + top_i
    gates = jnp.sum(jax.nn.one_hot(e_idx, N_EXPERTS, dtype=jnp.float32) * w_in[..., None], axis=1)
    a = jnp.einsum("td,edf->tef", t, w_gate)
    u = jnp.einsum("td,edf->tef", t, w_up)
    hid = jax.nn.silu(a) * u * gates[:, :, None].astype(t.dtype)
    y = jnp.einsum("tef,efd->td", hid, w_down)
    return y.reshape(B, S, D)


def setup_inputs(seed: int = 0) -> dict:
    key = jax.random.key(seed)
    ks = jax.random.split(key, 24)
    f32 = jnp.float32
    D = D_MODEL

    def nrm(k, shape, fan_in):
        return jax.random.normal(k, shape, f32) * (fan_in ** -0.5)

    def gain(k, shape):
        return 1.0 + 0.05 * jax.random.normal(k, shape, f32)

    def bias(k, shape):
        return 0.02 * jax.random.normal(k, shape, f32)

    return {
        "x": jax.random.normal(ks[0], (BATCH, SEQ, D), f32),
        "a_norm": gain(ks[1], (N_A_LAYERS, D)),
        "a_w_in": nrm(ks[2], (N_A_LAYERS, D, 2 * D), D),
        "a_b_in": bias(ks[3], (N_A_LAYERS, 2 * D)),
        "a_w_dw": nrm(ks[4], (N_A_LAYERS, CONV_KERNEL, D), CONV_KERNEL),
        "a_b_dw": bias(ks[5], (N_A_LAYERS, D)),
        "a_ln_g": gain(ks[6], (N_A_LAYERS, D)),
        "a_ln_b": bias(ks[7], (N_A_LAYERS, D)),
        "a_w_out": nrm(ks[8], (N_A_LAYERS, D, D), D),
        "a_b_out": bias(ks[9], (N_A_LAYERS, D)),
        "kv_norm": gain(ks[10], (D,)),
        "w_kv": nrm(ks[11], (D, 2 * N_DGROUPS * GROUP_WIDTH), D),
        "b_norm": gain(ks[12], (N_B_LAYERS, D)),
        "b_w_q": nrm(ks[13], (N_B_LAYERS, D, N_DGROUPS * GROUP_WIDTH), D),
        "b_w_o": nrm(ks[14], (N_B_LAYERS, GROUP_WIDTH, D), GROUP_WIDTH),
        "ffn_norm": gain(ks[15], (DEPTH, D)),
        "router_group": nrm(ks[16], (DEPTH, D, N_EXPERT_GROUPS), D),
        "router_expert": nrm(ks[17], (DEPTH, N_EXPERT_GROUPS, D, EXPERTS_PER_GROUP), D),
        "w_gate": nrm(ks[18], (DEPTH, N_EXPERTS, D, D_EXPERT), D),
        "w_up": nrm(ks[19], (DEPTH, N_EXPERTS, D, D_EXPERT), D),
        "w_down": nrm(ks[20], (DEPTH, N_EXPERTS, D_EXPERT, D), D_EXPERT),
        "final_norm": gain(ks[21], (D,)),
    }


def reference(x, a_norm, a_w_in, a_b_in, a_w_dw, a_b_dw, a_ln_g, a_ln_b, a_w_out, a_b_out,
              kv_norm, w_kv, b_norm, b_w_q, b_w_o, ffn_norm, router_group, router_expert,
              w_gate, w_up, w_down, final_norm):
    kv_blocks = None
    for l in range(DEPTH):
        if l < N_A_LAYERS:
            h = rms_norm(x, a_norm[l])
            x = x + conformer_conv(h, a_w_in[l], a_b_in[l], a_w_dw[l], a_b_dw[l],
                                   a_ln_g[l], a_ln_b[l], a_w_out[l], a_b_out[l])
        else:
            j = l - N_A_LAYERS
            h = rms_norm(x, b_norm[j])
            x = x + dilated_attention(h, b_w_q[j], b_w_o[j], kv_blocks)
        h = rms_norm(x, ffn_norm[l])
        x = x + hierarchical_moe(h, router_group[l], router_expert[l], w_gate[l], w_up[l], w_down[l])
        if l == N_A_LAYERS - 1:
            kv_blocks = shared_kv(x, kv_norm, w_kv)
    return rms_norm(x, final_norm)
```

```python
import functools
import math

import jax
import jax.numpy as jnp
import numpy as np
from jax import lax
from jax.experimental import pallas as pl
from jax.experimental.pallas import tpu as pltpu

D_MODEL = 1024
CONV_KERNEL = 31
HEAD_DIM = 64
N_HEADS = 16
DILATED_GROUPS = ((128, 1), (512, 4), (2048, 16))
N_DGROUPS = len(DILATED_GROUPS)
ATT_BLOCK = 128
ROPE_THETA = 10000.0
N_EXPERT_GROUPS = 4
EXPERTS_PER_GROUP = 4
N_EXPERTS = N_EXPERT_GROUPS * EXPERTS_PER_GROUP
D_EXPERT = D_MODEL // 2
EPS = 1e-6

V7X_LANES = 128
V7X_SUBLANES = 8
V7X_VMEM_LIMIT_BYTES = 56 * 1024 * 1024

SEQ_TILE = 512
CONV_HALO = 32
CONV_CHUNK = 32
ROW_SLABS = D_MODEL // V7X_LANES
EXPERT_TILE = 256
PERMUTE_CHUNK = 1024
N_PAIRS = EXPERTS_PER_GROUP * (EXPERTS_PER_GROUP - 1) // 2
N_CLASSES = N_EXPERT_GROUPS * N_PAIRS
CLASS_ROWS = 32
NEG_BIG = -1e30

_PAIRS = [(a, b) for a in range(EXPERTS_PER_GROUP) for b in range(a + 1, EXPERTS_PER_GROUP)]


def _cparams(*sem):
    return pltpu.CompilerParams(dimension_semantics=sem, vmem_limit_bytes=V7X_VMEM_LIMIT_BYTES)


def _rms(x, g):
    return x * lax.rsqrt(jnp.mean(x * x, axis=-1, keepdims=True) + EPS) * g


def _load_row8(ref, n):
    return jnp.concatenate([ref[pl.ds(s, n, stride=ROW_SLABS), :] for s in range(ROW_SLABS)], axis=1)


def _store_row8(ref, val, n):
    for s in range(ROW_SLABS):
        ref[pl.ds(s, n, stride=ROW_SLABS), :] = val[:, s * V7X_LANES:(s + 1) * V7X_LANES]


def _bf16_bits(x):
    return lax.bitcast_convert_type(x.astype(jnp.bfloat16).astype(jnp.float32), jnp.uint32)


def _pack_pair(lo, hi):
    return (_bf16_bits(lo) >> 16) | _bf16_bits(hi)


def _unpack_lo(w):
    return lax.bitcast_convert_type(w << 16, jnp.float32)


def _unpack_hi(w):
    return lax.bitcast_convert_type(w & jnp.uint32(0xFFFF0000), jnp.float32)


def _a_layer_kernel(has_y, *refs):
    if has_y:
        x_ref, y_ref, *refs = refs
    else:
        x_ref, *refs = refs
    (g_ref, win_ref, bin_ref, wdw_ref, bdw_ref, lng_ref, lnb_ref, wout_ref, bout_ref, o_ref,
     ext_ref, sh_ref, conv_ref) = refs
    ts, d = x_ref.shape

    @pl.when(pl.program_id(1) == 0)
    def _():
        ext_ref[0:CONV_HALO, :] = jnp.zeros((CONV_HALO, d), jnp.float32)

    x = x_ref[...]
    if has_y:
        x = x + _load_row8(y_ref, ts)
    h = _rms(x, g_ref[...])
    u = jnp.dot(h.astype(jnp.bfloat16), win_ref[...], preferred_element_type=jnp.float32) + bin_ref[...]
    ext_ref[CONV_HALO:CONV_HALO + ts, :] = u[:, :d] * jax.nn.sigmoid(u[:, d:])

    base = CONV_HALO - (CONV_KERNEL - 1)
    nsh = ts + CONV_HALO - V7X_SUBLANES
    for j in range(1, V7X_SUBLANES):
        sh_ref[j - 1] = ext_ref[pl.ds(j, nsh), :]

    def chunk(c, carry):
        r = pl.multiple_of(c * CONV_CHUNK, CONV_CHUNK)
        acc = jnp.broadcast_to(bdw_ref[...], (CONV_CHUNK, d))
        for k in range(CONV_KERNEL):
            q, j = divmod(base + k, V7X_SUBLANES)
            rows = pl.ds(r + q * V7X_SUBLANES, CONV_CHUNK)
            tap = ext_ref[rows, :] if j == 0 else sh_ref[j - 1, rows, :]
            acc = acc + wdw_ref[k:k + 1, :] * tap
        conv_ref[pl.ds(r, CONV_CHUNK), :] = acc
        return carry

    lax.fori_loop(0, ts // CONV_CHUNK, chunk, 0)
    ext_ref[0:CONV_HALO, :] = ext_ref[ts:ts + CONV_HALO, :]

    c = conv_ref[...]
    mu = jnp.mean(c, axis=-1, keepdims=True)
    cc = c - mu
    var = jnp.mean(cc * cc, axis=-1, keepdims=True)
    v = cc * lax.rsqrt(var + EPS) * lng_ref[...] + lnb_ref[...]
    v = v * jax.nn.sigmoid(v)
    out = jnp.dot(v.astype(jnp.bfloat16), wout_ref[...], preferred_element_type=jnp.float32)
    o_ref[...] = x + out + bout_ref[...]


def _row(v):
    return v.reshape(1, -1)


def _a_layer(x2, y8, batch, g, w_in, b_in, w_dw, b_dw, ln_g, ln_b, w_out, b_out):
    t, d = x2.shape
    ns = t // batch // SEQ_TILE
    const = lambda shape: pl.BlockSpec(shape, lambda b, s: (0, 0))
    tok = pl.BlockSpec((SEQ_TILE, d), lambda b, s: (b * ns + s, 0))
    acts, act_specs = [x2], [tok]
    if y8 is not None:
        acts.append(y8)
        act_specs.append(pl.BlockSpec((SEQ_TILE * ROW_SLABS, V7X_LANES), lambda b, s: (b * ns + s, 0)))
    return pl.pallas_call(
        functools.partial(_a_layer_kernel, y8 is not None),
        grid=(batch, ns),
        in_specs=act_specs + [
            const((1, d)), const((d, 2 * d)), const((1, 2 * d)), const((CONV_KERNEL, d)), const((1, d)),
            const((1, d)), const((1, d)), const((d, d)), const((1, d)),
        ],
        out_specs=tok,
        out_shape=jax.ShapeDtypeStruct((t, d), jnp.float32),
        scratch_shapes=[pltpu.VMEM((CONV_HALO + SEQ_TILE, d), jnp.float32),
                        pltpu.VMEM((V7X_SUBLANES - 1, CONV_HALO + SEQ_TILE - V7X_SUBLANES, d), jnp.float32),
                        pltpu.VMEM((SEQ_TILE, d), jnp.float32)],
        compiler_params=_cparams("arbitrary", "arbitrary"),
        name="a_layer",
    )(*acts, _row(g), w_in.astype(jnp.bfloat16), _row(b_in), w_dw, _row(b_dw), _row(ln_g), _row(ln_b),
      w_out.astype(jnp.bfloat16), _row(b_out))


def _route_kernel(x_ref, g_ref, wr_ref, hp_ref, cls_ref, rank_ref, cnt_ref, carry_ref):
    tr = x_ref.shape[0]

    @pl.when(pl.program_id(0) == 0)
    def _():
        carry_ref[...] = jnp.zeros_like(carry_ref)

    h = _rms(x_ref[...], g_ref[...])
    lt = lax.dot_general(wr_ref[...], h, (((1,), (1,)), ((), ())), precision=lax.Precision.HIGHEST,
                         preferred_element_type=jnp.float32)
    lg = [lt[i:i + 1, :] for i in range(N_EXPERT_GROUPS)]
    gmax = functools.reduce(jnp.maximum, lg)
    g_idx = jnp.full_like(gmax, N_EXPERT_GROUPS - 1, dtype=jnp.int32)
    for i in reversed(range(N_EXPERT_GROUPS - 1)):
        g_idx = jnp.where(lg[i] == gmax, i, g_idx)
    p_top = 1.0 / functools.reduce(jnp.add, [jnp.exp(v - gmax) for v in lg])
    sel = []
    for e in range(EXPERTS_PER_GROUP):
        v = lt[N_EXPERT_GROUPS + e:N_EXPERT_GROUPS + e + 1, :]
        for g in range(1, N_EXPERT_GROUPS):
            r = N_EXPERT_GROUPS + g * EXPERTS_PER_GROUP + e
            v = jnp.where(g_idx == g, lt[r:r + 1, :], v)
        sel.append(v)
    v1 = functools.reduce(jnp.maximum, sel)
    i1 = jnp.full_like(g_idx, EXPERTS_PER_GROUP - 1)
    for e in reversed(range(EXPERTS_PER_GROUP - 1)):
        i1 = jnp.where(sel[e] == v1, e, i1)
    rest = [jnp.where(i1 == e, -jnp.inf, sel[e]) for e in range(EXPERTS_PER_GROUP)]
    v2 = functools.reduce(jnp.maximum, rest)
    i2 = jnp.full_like(g_idx, EXPERTS_PER_GROUP - 1)
    for e in reversed(range(EXPERTS_PER_GROUP - 1)):
        i2 = jnp.where((rest[e] == v2) & (i1 != e), e, i2)
    i2 = jnp.where((i2 == i1), jnp.where(i1 == EXPERTS_PER_GROUP - 1, EXPERTS_PER_GROUP - 2, i2), i2)
    t2 = jnp.exp(v2 - v1)
    w1 = p_top / (1.0 + t2)
    w2 = p_top * t2 / (1.0 + t2)
    first_lo = i1 < i2
    e_lo = jnp.where(first_lo, i1, i2)
    e_hi = jnp.where(first_lo, i2, i1)
    w_lo = jnp.where(first_lo, w1, w2)
    w_hi = jnp.where(first_lo, w2, w1)
    pair_base = jnp.where(e_lo == 0, 0, jnp.where(e_lo == 1, EXPERTS_PER_GROUP - 1, 2 * EXPERTS_PER_GROUP - 3))
    cls = g_idx * N_PAIRS + pair_base + (e_hi - e_lo - 1)

    onehot = (lax.broadcasted_iota(jnp.int32, (CLASS_ROWS, tr), 0) == cls)
    before = (lax.broadcasted_iota(jnp.int32, (tr, tr), 0) < lax.broadcasted_iota(jnp.int32, (tr, tr), 1))
    cum = jnp.dot(onehot.astype(jnp.bfloat16), before.astype(jnp.bfloat16), preferred_element_type=jnp.float32)
    oh = onehot.astype(jnp.float32)
    rank = jnp.sum(oh * (cum + carry_ref[...]), axis=0, keepdims=True)
    carry_ref[...] = carry_ref[...] + jnp.sum(oh, axis=1, keepdims=True)
    cls_ref[0] = cls
    rank_ref[0] = rank.astype(jnp.int32)
    cnt_ref[...] = jnp.broadcast_to(carry_ref[...], cnt_ref.shape)

    half = D_MODEL // 2
    word = _pack_pair(h[:, :half], h[:, half:])
    n_word = half // V7X_LANES
    for s in range(n_word):
        hp_ref[pl.ds(s, tr, stride=ROW_SLABS), :] = word[:, s * V7X_LANES:(s + 1) * V7X_LANES]
    wcols = jnp.concatenate([w_lo, w_hi, jnp.zeros((V7X_LANES - 2, tr), jnp.float32)], axis=0).T
    hp_ref[pl.ds(n_word, tr, stride=ROW_SLABS), :] = lax.bitcast_convert_type(wcols, jnp.uint32)
    for s in range(n_word + 1, ROW_SLABS):
        hp_ref[pl.ds(s, tr, stride=ROW_SLABS), :] = jnp.zeros((tr, V7X_LANES), jnp.uint32)


def _moe_route(x2, g, r_grp, r_exp):
    t, d = x2.shape
    nt = t // SEQ_TILE
    wr = jnp.concatenate([r_grp, jnp.transpose(r_exp, (1, 0, 2)).reshape(d, N_EXPERTS)], axis=1)
    wr = jnp.pad(wr, ((0, 0), (0, CLASS_ROWS - wr.shape[1]))).T
    vec = lambda dt: jax.ShapeDtypeStruct((nt, 1, SEQ_TILE), dt)
    vec_spec = pl.BlockSpec((1, 1, SEQ_TILE), lambda i: (i, 0, 0))
    return pl.pallas_call(
        _route_kernel,
        grid=(nt,),
        in_specs=[pl.BlockSpec((SEQ_TILE, d), lambda i: (i, 0)),
                  pl.BlockSpec((1, d), lambda i: (0, 0)),
                  pl.BlockSpec((CLASS_ROWS, d), lambda i: (0, 0))],
        out_specs=[pl.BlockSpec((SEQ_TILE * ROW_SLABS, V7X_LANES), lambda i: (i, 0)), vec_spec, vec_spec,
                   pl.BlockSpec((CLASS_ROWS, V7X_LANES), lambda i: (0, 0))],
        out_shape=[jax.ShapeDtypeStruct((t * ROW_SLABS, V7X_LANES), jnp.uint32), vec(jnp.int32), vec(jnp.int32),
                   jax.ShapeDtypeStruct((CLASS_ROWS, V7X_LANES), jnp.float32)],
        scratch_shapes=[pltpu.VMEM((CLASS_ROWS, 1), jnp.float32)],
        compiler_params=_cparams("arbitrary"),
        name="moe_route",
    )(x2, _row(g), wr)


def _row_copy(src_ref, dst_ref, sem, src_row, dst_row):
    return pltpu.make_async_copy(
        src_ref.at[pl.ds(pl.multiple_of(src_row * ROW_SLABS, ROW_SLABS), ROW_SLABS)],
        dst_ref.at[pl.ds(pl.multiple_of(dst_row * ROW_SLABS, ROW_SLABS), ROW_SLABS)], sem)


def _permute_kernel(scatter, *refs):
    if scatter:
        pos_ref, src_ref, _, dst_ref, sem = refs
    else:
        pos_ref, src_ref, dst_ref, sem = refs
    t0 = pl.program_id(0) * PERMUTE_CHUNK

    def issue(k, carry):
        p = pos_ref[0, 0, k]
        src_row, dst_row = (t0 + k, p) if scatter else (p, t0 + k)
        _row_copy(src_ref, dst_ref, sem, src_row, dst_row).start()
        return carry

    lax.fori_loop(0, PERMUTE_CHUNK, issue, 0)

    def drain(k, carry):
        _row_copy(src_ref, dst_ref, sem, 0, 0).wait()
        return carry

    lax.fori_loop(0, PERMUTE_CHUNK, drain, 0)


def _row_permute(src, pos, n_dst_rows, dst_init=None):
    t = pos.shape[0]
    nchunk = t // PERMUTE_CHUNK
    scatter = dst_init is not None
    any_spec = pl.BlockSpec(memory_space=pl.ANY)
    args = [pos.reshape(nchunk, 1, PERMUTE_CHUNK), src] + ([dst_init] if scatter else [])
    return pl.pallas_call(
        functools.partial(_permute_kernel, scatter),
        grid=(nchunk,),
        in_specs=[pl.BlockSpec((1, 1, PERMUTE_CHUNK), lambda i: (i, 0, 0), memory_space=pltpu.SMEM), any_spec]
                 + ([any_spec] if scatter else []),
        out_specs=any_spec,
        out_shape=jax.ShapeDtypeStruct((n_dst_rows * ROW_SLABS, V7X_LANES), src.dtype),
        scratch_shapes=[pltpu.SemaphoreType.DMA],
        input_output_aliases=({2: 0} if scatter else {}),
        compiler_params=_cparams("arbitrary"),
        name="row_scatter" if scatter else "row_gather",
    )(*args)


def _expert_kernel(lo_ref, hi_ref, nt_ref, hs_ref, wg_lo, wu_lo, wd_lo, wg_hi, wu_hi, wd_hi, y_ref):
    tm = EXPERT_TILE
    j = pl.program_id(0)

    @pl.when(j < nt_ref[0])
    def _():
        n_word = D_MODEL // 2 // V7X_LANES
        words = [hs_ref[pl.ds(s, tm, stride=ROW_SLABS), :] for s in range(n_word)]
        h = jnp.concatenate([_unpack_lo(w) for w in words] + [_unpack_hi(w) for w in words], axis=1)
        h = h.astype(jnp.bfloat16)
        gates = lax.bitcast_convert_type(hs_ref[pl.ds(n_word, tm, stride=ROW_SLABS), :], jnp.float32)
        y = None
        for col, (wg, wu, wd) in enumerate(((wg_lo, wu_lo, wd_lo), (wg_hi, wu_hi, wd_hi))):
            a = jnp.dot(h, wg[0], preferred_element_type=jnp.float32)
            u = jnp.dot(h, wu[0], preferred_element_type=jnp.float32)
            hid = a * jax.nn.sigmoid(a) * u * gates[:, col:col + 1]
            part = jnp.dot(hid.astype(jnp.bfloat16), wd[0], preferred_element_type=jnp.float32)
            y = part if y is None else y + part
        _store_row8(y_ref, y, tm)

    @pl.when(j >= nt_ref[0])
    def _():
        y_ref[...] = jnp.zeros_like(y_ref)


def _moe_experts(hs, tile_lo, tile_hi, n_tiles, w_gate, w_up, w_down):
    n_rows = hs.shape[0] // ROW_SLABS
    nt_max = n_rows // EXPERT_TILE
    d, f = w_gate.shape[1], w_gate.shape[2]
    act = lambda j, lo, hi, nt: (jnp.minimum(j, nt[0] - 1), 0)
    w_lo = lambda j, lo, hi, nt: (lo[j], 0, 0)
    w_hi = lambda j, lo, hi, nt: (hi[j], 0, 0)
    up_spec = lambda m: pl.BlockSpec((1, d, f), m)
    dn_spec = lambda m: pl.BlockSpec((1, f, d), m)
    blk = (EXPERT_TILE * ROW_SLABS, V7X_LANES)
    wg, wu, wd = (w.astype(jnp.bfloat16) for w in (w_gate, w_up, w_down))
    return pl.pallas_call(
        _expert_kernel,
        grid_spec=pltpu.PrefetchScalarGridSpec(
            num_scalar_prefetch=3,
            grid=(nt_max,),
            in_specs=[pl.BlockSpec(blk, act), up_spec(w_lo), up_spec(w_lo), dn_spec(w_lo),
                      up_spec(w_hi), up_spec(w_hi), dn_spec(w_hi)],
            out_specs=pl.BlockSpec(blk, lambda j, lo, hi, nt: (j, 0)),
        ),
        out_shape=jax.ShapeDtypeStruct((n_rows * ROW_SLABS, V7X_LANES), jnp.float32),
        compiler_params=_cparams("arbitrary"),
        name="moe_experts",
    )(tile_lo, tile_hi, n_tiles, hs, wg, wu, wd, wg, wu, wd)


def _moe(x2, g, r_grp, r_exp, w_gate, w_up, w_down):
    t = x2.shape[0]
    hp, cls, rank, cnt = _moe_route(x2, g, r_grp, r_exp)
    counts = cnt[:N_CLASSES, 0].astype(jnp.int32)
    tiles_per = (counts + EXPERT_TILE - 1) // EXPERT_TILE
    tile_end = jnp.cumsum(tiles_per)
    base = (tile_end - tiles_per) * EXPERT_TILE
    cls = cls.reshape(t)
    pos = jnp.sum(jnp.where(cls[:, None] == jnp.arange(N_CLASSES)[None, :], base[None, :], 0), axis=1)
    pos = (pos + rank.reshape(t)).astype(jnp.int32)
    nt_max = t // EXPERT_TILE + N_CLASSES
    tile_cls = jnp.sum(jnp.arange(nt_max)[:, None] >= tile_end[None, :], axis=1)
    tile_cls = jnp.minimum(tile_cls, N_CLASSES - 1)
    lo_tab = jnp.array([p[0] for p in _PAIRS], jnp.int32)
    hi_tab = jnp.array([p[1] for p in _PAIRS], jnp.int32)
    tile_g = tile_cls // N_PAIRS
    tile_lo = (tile_g * EXPERTS_PER_GROUP + lo_tab[tile_cls % N_PAIRS]).astype(jnp.int32)
    tile_hi = (tile_g * EXPERTS_PER_GROUP + hi_tab[tile_cls % N_PAIRS]).astype(jnp.int32)
    n_tiles = tile_end[-1:].astype(jnp.int32)
    n_sorted = nt_max * EXPERT_TILE
    hs = _row_permute(hp, pos, n_sorted, dst_init=jnp.zeros((n_sorted * ROW_SLABS, V7X_LANES), jnp.uint32))
    ys = _moe_experts(hs, tile_lo, tile_hi, n_tiles, w_gate, w_up, w_down)
    return _row_permute(ys, pos, t)


def _rope_tables(seq):
    pos = jnp.arange(seq, dtype=jnp.float32)
    inv_freq = ROPE_THETA ** (-jnp.arange(0, HEAD_DIM, 2, dtype=jnp.float32) / HEAD_DIM)
    ang = pos[:, None] * inv_freq[None, :]
    reps = V7X_LANES // (HEAD_DIM // 2)
    cos = jnp.tile(jnp.cos(ang), (1, reps))
    sign = jnp.tile(jnp.concatenate([-jnp.ones(HEAD_DIM // 2), jnp.ones(HEAD_DIM // 2)]), V7X_LANES // HEAD_DIM)
    sin = jnp.tile(jnp.sin(ang), (1, reps)) * sign[None, :]
    return cos, sin


def _rope_slab(v, cos, sin, lane):
    half = HEAD_DIM // 2
    partner = jnp.where((lane % HEAD_DIM) < half, pltpu.roll(v, V7X_LANES - half, axis=1), pltpu.roll(v, half, axis=1))
    return v * cos + partner * sin


def _proj_kernel(has_y, specs, *refs):
    n = len(specs)
    if has_y:
        x_ref, y_ref, *refs = refs
    else:
        x_ref, *refs = refs
    cos_ref, sin_ref, *refs = refs
    g_refs, w_refs, refs = refs[:n], refs[n:2 * n], refs[2 * n:]
    if has_y:
        xo_ref, *o_refs = refs
    else:
        o_refs = refs
    ts = x_ref.shape[0]
    x = x_ref[...]
    if has_y:
        x = x + _load_row8(y_ref, ts)
        xo_ref[...] = x
    inv = lax.rsqrt(jnp.mean(x * x, axis=-1, keepdims=True) + EPS)
    xn = x * inv
    cos, sin = cos_ref[...], sin_ref[...]
    lane = lax.broadcasted_iota(jnp.int32, (ts, V7X_LANES), 1)
    hn_cache = {}
    for i, (use_rope, scale) in enumerate(specs):
        gi = id(g_refs[i])
        h = (xn * g_refs[i][...]).astype(jnp.bfloat16)
        for grp in range(N_DGROUPS):
            acc = jnp.dot(h, w_refs[i][:, grp * D_MODEL:(grp + 1) * D_MODEL], preferred_element_type=jnp.float32)
            slabs = []
            for s in range(ROW_SLABS):
                v = acc[:, s * V7X_LANES:(s + 1) * V7X_LANES]
                if use_rope:
                    v = _rope_slab(v, cos, sin, lane)
                if scale != 1.0:
                    v = v * scale
                slabs.append(v)
            for q in range(ROW_SLABS // 2):
                o_refs[i][grp, :, q * V7X_LANES:(q + 1) * V7X_LANES] = _pack_pair(slabs[2 * q], slabs[2 * q + 1])


def _qkv_proj(x2, y8, seq, projections):
    t, d = x2.shape
    ns = seq // SEQ_TILE
    n = len(projections)
    cos, sin = _rope_tables(seq)
    tok = pl.BlockSpec((SEQ_TILE, d), lambda i: (i, 0))
    rope_spec = pl.BlockSpec((SEQ_TILE, V7X_LANES), lambda i: (i % ns, 0))
    acts, act_specs = [x2], [tok]
    if y8 is not None:
        acts.append(y8)
        act_specs.append(pl.BlockSpec((SEQ_TILE * ROW_SLABS, V7X_LANES), lambda i: (i, 0)))
    packed = jax.ShapeDtypeStruct((N_DGROUPS, t, d // 2), jnp.uint32)
    packed_spec = pl.BlockSpec((N_DGROUPS, SEQ_TILE, d // 2), lambda i: (0, i, 0))
    out_shape, out_specs = [packed] * n, [packed_spec] * n
    if y8 is not None:
        out_shape = [jax.ShapeDtypeStruct((t, d), jnp.float32)] + out_shape
        out_specs = [tok] + out_specs
    specs = tuple((p[2], p[3]) for p in projections)
    return pl.pallas_call(
        functools.partial(_proj_kernel, y8 is not None, specs),
        grid=(t // SEQ_TILE,),
        in_specs=act_specs + [rope_spec, rope_spec]
                 + [pl.BlockSpec((1, d), lambda i: (0, 0))] * n
                 + [pl.BlockSpec((d, N_DGROUPS * d), lambda i: (0, 0), pipeline_mode=pl.Buffered(1))] * n,
        out_specs=out_specs,
        out_shape=out_shape,
        compiler_params=_cparams("arbitrary"),
        name="qkv_proj",
    )(*acts, cos, sin, *[_row(p[0]) for p in projections], *[p[1].astype(jnp.bfloat16) for p in projections])


def _attn_unit(pair, qw, kw, vw, keep, lane):
    unpack = _unpack_lo if pair == 0 else _unpack_hi
    q = unpack(qw).astype(jnp.bfloat16)
    k = unpack(kw).astype(jnp.bfloat16)
    v = unpack(vw).astype(jnp.bfloat16)
    first = lane < HEAD_DIM
    zero = jnp.zeros_like(q)
    qs = jnp.concatenate([jnp.where(first, q, zero), jnp.where(first, zero, q)], axis=0)
    s = lax.dot_general(qs, k, (((1,), (1,)), ((), ())), preferred_element_type=jnp.float32)
    s = jnp.where(keep, s, NEG_BIG)
    m = jnp.max(s, axis=1, keepdims=True)
    p = jnp.exp(s - m)
    l = jnp.sum(p, axis=1, keepdims=True)
    pv = jnp.dot(p.astype(jnp.bfloat16), v, preferred_element_type=jnp.float32)
    nq = ATT_BLOCK
    o = jnp.where(first, pv[:nq], pv[nq:])
    shape = (nq, V7X_LANES)
    m2 = jnp.where(first, jnp.broadcast_to(m[:nq], shape), jnp.broadcast_to(m[nq:], shape))
    l2 = jnp.where(first, jnp.broadcast_to(l[:nq], shape), jnp.broadcast_to(l[nq:], shape))
    return o, m2, l2


def _attn_kernel(q_ref, k_ref, v_ref, o_ref, acc_ref, m_ref, l_ref):
    seq = q_ref.shape[1]
    nq = ATT_BLOCK
    lane = lax.broadcasted_iota(jnp.int32, (nq, V7X_LANES), 1)
    qi = lax.broadcasted_iota(jnp.int32, (2 * nq, 2 * nq), 0) % nq
    kj = lax.broadcasted_iota(jnp.int32, (2 * nq, 2 * nq), 1)
    band_prev = (kj < nq) & (kj >= qi)
    band_cur = (kj >= nq) & (kj - nq <= qi)
    causal = band_cur[:, nq:]

    for grp, (_, dil) in enumerate(DILATED_GROUPS):
        nblk = seq // dil // nq

        def rows(start):
            return pl.ds(pl.multiple_of(start, nq), nq) if dil == 1 else pl.ds(start, nq, stride=dil)

        def unit(idx, carry):
            r, n = idx // nblk, idx % nblk
            start = n * nq * dil + r
            qw = q_ref[grp, rows(start), :]
            kc, vc = k_ref[grp, rows(start), :], v_ref[grp, rows(start), :]
            if nblk > 1:
                prev = jnp.maximum(n - 1, 0) * nq * dil + r
                kw = jnp.concatenate([k_ref[grp, rows(prev), :], kc], axis=0)
                vw = jnp.concatenate([v_ref[grp, rows(prev), :], vc], axis=0)
                keep = band_cur | (band_prev & (n > 0))
            else:
                kw, vw, keep = kc, vc, causal
            for pair in range(2):
                o, m2, l2 = _attn_unit(pair, qw, kw, vw, keep, lane)
                if grp == 0:
                    acc_ref[pair, rows(start), :] = o
                    m_ref[pair, rows(start), :] = m2
                    l_ref[pair, rows(start), :] = l2
                else:
                    m_old = m_ref[pair, rows(start), :]
                    m_new = jnp.maximum(m_old, m2)
                    a_old, a_new = jnp.exp(m_old - m_new), jnp.exp(m2 - m_new)
                    acc_ref[pair, rows(start), :] = acc_ref[pair, rows(start), :] * a_old + o * a_new
                    l_ref[pair, rows(start), :] = l_ref[pair, rows(start), :] * a_old + l2 * a_new
                    m_ref[pair, rows(start), :] = m_new
            return carry

        lax.fori_loop(0, seq // nq, unit, 0)

    for pair in range(2):
        o_ref[:, pair * V7X_LANES:(pair + 1) * V7X_LANES] = (acc_ref[pair] / l_ref[pair]).astype(o_ref.dtype)


def _dilated_attention(qp, kp, vp, batch):
    t = qp.shape[1]
    seq = t // batch
    nquad = qp.shape[2] // V7X_LANES
    spec = pl.BlockSpec((N_DGROUPS, seq, V7X_LANES), lambda b, c: (0, b, c))
    return pl.pallas_call(
        _attn_kernel,
        grid=(batch, nquad),
        in_specs=[spec, spec, spec],
        out_specs=pl.BlockSpec((seq, 2 * V7X_LANES), lambda b, c: (b, c)),
        out_shape=jax.ShapeDtypeStruct((t, D_MODEL), jnp.bfloat16),
        scratch_shapes=[pltpu.VMEM((2, seq, V7X_LANES), jnp.float32)] * 3,
        compiler_params=_cparams("arbitrary", "arbitrary"),
        name="dilated_attention",
    )(qp, kp, vp)


def _attn_out_kernel(x_ref, c_ref, w_ref, o_ref):
    o_ref[...] = x_ref[...] + jnp.dot(c_ref[...], w_ref[...], preferred_element_type=jnp.float32)


def _attn_out(x2, comb, w_o):
    t, d = x2.shape
    tok = pl.BlockSpec((SEQ_TILE, d), lambda i: (i, 0))
    return pl.pallas_call(
        _attn_out_kernel,
        grid=(t // SEQ_TILE,),
        in_specs=[tok, tok, pl.BlockSpec((d, d), lambda i: (0, 0))],
        out_specs=tok,
        out_shape=jax.ShapeDtypeStruct((t, d), jnp.float32),
        compiler_params=_cparams("arbitrary"),
        name="attn_out",
    )(x2, comb, w_o.astype(jnp.bfloat16))


def _final_kernel(x_ref, y_ref, g_ref, o_ref):
    x = x_ref[...] + _load_row8(y_ref, x_ref.shape[0])
    o_ref[...] = _rms(x, g_ref[...])


def _final_norm(x2, y8, g):
    t, d = x2.shape
    tok = pl.BlockSpec((SEQ_TILE, d), lambda i: (i, 0))
    return pl.pallas_call(
        _final_kernel,
        grid=(t // SEQ_TILE,),
        in_specs=[tok, pl.BlockSpec((SEQ_TILE * ROW_SLABS, V7X_LANES), lambda i: (i, 0)),
                  pl.BlockSpec((1, d), lambda i: (0, 0))],
        out_specs=tok,
        out_shape=jax.ShapeDtypeStruct((t, d), jnp.float32),
        compiler_params=_cparams("arbitrary"),
        name="final_norm",
    )(x2, y8, _row(g))


def kernel(x, a_norm, a_w_in, a_b_in, a_w_dw, a_b_dw, a_ln_g, a_ln_b, a_w_out, a_b_out, kv_norm, w_kv, b_norm,
           b_w_q, b_w_o, ffn_norm, router_group, router_expert, w_gate, w_up, w_down, final_norm):
    batch, seq, d = x.shape
    n_a = a_norm.shape[0]
    depth = ffn_norm.shape[0]
    gw = N_DGROUPS * d
    q_scale = HEAD_DIM ** -0.5
    x2 = x.reshape(batch * seq, d)
    y8 = None
    kp = vp = None
    for l in range(depth):
        if l < n_a:
            x2 = _a_layer(x2, y8, batch, a_norm[l], a_w_in[l], a_b_in[l], a_w_dw[l], a_b_dw[l], a_ln_g[l],
                          a_ln_b[l], a_w_out[l], a_b_out[l])
        else:
            j = l - n_a
            projections = [(b_norm[j], b_w_q[j], True, q_scale)]
            if kp is None:
                projections += [(kv_norm, w_kv[:, :gw], True, 1.0), (kv_norm, w_kv[:, gw:], False, 1.0)]
            outs = _qkv_proj(x2, y8, seq, projections)
            if y8 is not None:
                x2, *outs = outs
            if kp is None:
                qp, kp, vp = outs
            else:
                qp, = outs
            comb = _dilated_attention(qp, kp, vp, batch)
            x2 = _attn_out(x2, comb, b_w_o[j])
        y8 = _moe(x2, ffn_norm[l], router_group[l], router_expert[l], w_gate[l], w_up[l], w_down[l])
    return _final_norm(x2, y8, final_norm).reshape(batch, seq, d)
```

```python
import functools
import math

import jax
import jax.numpy as jnp
import numpy as np
from jax import lax
from jax.experimental import pallas as pl
from jax.experimental.pallas import tpu as pltpu

D_MODEL = 1024
CONV_KERNEL = 31
HEAD_DIM = 64
N_HEADS = 16
DILATED_GROUPS = ((128, 1), (512, 4), (2048, 16))
N_DGROUPS = len(DILATED_GROUPS)
ATT_BLOCK = 128
ROPE_THETA = 10000.0
N_EXPERT_GROUPS = 4
EXPERTS_PER_GROUP = 4
N_EXPERTS = N_EXPERT_GROUPS * EXPERTS_PER_GROUP
D_EXPERT = D_MODEL // 2
EPS = 1e-6

V7X_LANES = 128
V7X_SUBLANES = 8
V7X_VMEM_LIMIT_BYTES = 56 * 1024 * 1024

SEQ_TILE = 512
CONV_HALO = 32
CONV_CHUNK = 32
ROW_SLABS = D_MODEL // V7X_LANES
EXPERT_TILE = 256
ATT_UNROLL = 4
N_PAIRS = EXPERTS_PER_GROUP * (EXPERTS_PER_GROUP - 1) // 2
N_CLASSES = N_EXPERT_GROUPS * N_PAIRS
CLASS_ROWS = 32
NEG_BIG = -1e30

_PAIRS = [(a, b) for a in range(EXPERTS_PER_GROUP) for b in range(a + 1, EXPERTS_PER_GROUP)]


def _cparams(*sem):
    return pltpu.CompilerParams(dimension_semantics=sem, vmem_limit_bytes=V7X_VMEM_LIMIT_BYTES)


def _rms(x, g):
    return x * lax.rsqrt(jnp.mean(x * x, axis=-1, keepdims=True) + EPS) * g


def _load_row8(ref, n):
    return jnp.concatenate([ref[pl.ds(s, n, stride=ROW_SLABS), :] for s in range(ROW_SLABS)], axis=1)


def _store_row8(ref, val, n):
    for s in range(ROW_SLABS):
        ref[pl.ds(s, n, stride=ROW_SLABS), :] = val[:, s * V7X_LANES:(s + 1) * V7X_LANES]


def _bf16_bits(x):
    return lax.bitcast_convert_type(x.astype(jnp.bfloat16).astype(jnp.float32), jnp.uint32)


def _pack_pair(lo, hi):
    return (_bf16_bits(lo) >> 16) | _bf16_bits(hi)


def _unpack_lo(w):
    return lax.bitcast_convert_type(w << 16, jnp.float32)


def _unpack_hi(w):
    return lax.bitcast_convert_type(w & jnp.uint32(0xFFFF0000), jnp.float32)


def _a_layer_kernel(has_y, *refs):
    if has_y:
        x_ref, ys_ref, pos_ref, pos_next_ref, *refs = refs
    else:
        x_ref, *refs = refs
    (g_ref, win_ref, bin_ref, wdw_ref, bdw_ref, lng_ref, lnb_ref, wout_ref, bout_ref, o_ref,
     ext_ref, sh_ref, conv_ref, *gather_scratch) = refs
    ts, d = x_ref.shape

    @pl.when(pl.program_id(1) == 0)
    def _():
        ext_ref[0:CONV_HALO, :] = jnp.zeros((CONV_HALO, d), jnp.float32)

    x = x_ref[...]
    if has_y:
        step = pl.program_id(0) * pl.num_programs(1) + pl.program_id(1)
        n_steps = pl.num_programs(0) * pl.num_programs(1)
        x = x + _gathered_rows(step, n_steps, ys_ref, pos_ref, pos_next_ref, *gather_scratch)
    h = _rms(x, g_ref[...])
    u = jnp.dot(h.astype(jnp.bfloat16), win_ref[...], preferred_element_type=jnp.float32) + bin_ref[...]
    ext_ref[CONV_HALO:CONV_HALO + ts, :] = u[:, :d] * jax.nn.sigmoid(u[:, d:])

    base = CONV_HALO - (CONV_KERNEL - 1)
    nsh = ts + CONV_HALO - V7X_SUBLANES
    for j in range(1, V7X_SUBLANES):
        sh_ref[j - 1] = ext_ref[pl.ds(j, nsh), :]

    def chunk(c, carry):
        r = pl.multiple_of(c * CONV_CHUNK, CONV_CHUNK)
        acc = jnp.broadcast_to(bdw_ref[...], (CONV_CHUNK, d))
        for k in range(CONV_KERNEL):
            q, j = divmod(base + k, V7X_SUBLANES)
            rows = pl.ds(r + q * V7X_SUBLANES, CONV_CHUNK)
            tap = ext_ref[rows, :] if j == 0 else sh_ref[j - 1, rows, :]
            acc = acc + wdw_ref[k:k + 1, :] * tap
        conv_ref[pl.ds(r, CONV_CHUNK), :] = acc
        return carry

    lax.fori_loop(0, ts // CONV_CHUNK, chunk, 0)
    ext_ref[0:CONV_HALO, :] = ext_ref[ts:ts + CONV_HALO, :]

    c = conv_ref[...]
    mu = jnp.mean(c, axis=-1, keepdims=True)
    cc = c - mu
    var = jnp.mean(cc * cc, axis=-1, keepdims=True)
    v = cc * lax.rsqrt(var + EPS) * lng_ref[...] + lnb_ref[...]
    v = v * jax.nn.sigmoid(v)
    out = jnp.dot(v.astype(jnp.bfloat16), wout_ref[...], preferred_element_type=jnp.float32)
    o_ref[...] = x + out + bout_ref[...]


def _row(v):
    return v.reshape(1, -1)


def _a_layer(x2, moe, batch, g, w_in, b_in, w_dw, b_dw, ln_g, ln_b, w_out, b_out):
    t, d = x2.shape
    ns = t // batch // SEQ_TILE
    const = lambda shape: pl.BlockSpec(shape, lambda b, s: (0, 0))
    tok = pl.BlockSpec((SEQ_TILE, d), lambda b, s: (b * ns + s, 0))
    acts, act_specs, gather_scratch = [x2], [tok], []
    if moe is not None:
        g_args, g_specs, gather_scratch = _gather_operands(moe, batch * ns, lambda b, s: b * ns + s)
        acts += g_args
        act_specs += g_specs
    return pl.pallas_call(
        functools.partial(_a_layer_kernel, moe is not None),
        grid=(batch, ns),
        in_specs=act_specs + [
            const((1, d)), const((d, 2 * d)), const((1, 2 * d)), const((CONV_KERNEL, d)), const((1, d)),
            const((1, d)), const((1, d)), const((d, d)), const((1, d)),
        ],
        out_specs=tok,
        out_shape=jax.ShapeDtypeStruct((t, d), jnp.float32),
        scratch_shapes=[pltpu.VMEM((CONV_HALO + SEQ_TILE, d), jnp.float32),
                        pltpu.VMEM((V7X_SUBLANES - 1, CONV_HALO + SEQ_TILE - V7X_SUBLANES, d), jnp.float32),
                        pltpu.VMEM((SEQ_TILE, d), jnp.float32)] + gather_scratch,
        compiler_params=_cparams("arbitrary", "arbitrary"),
        name="a_layer",
    )(*acts, _row(g), w_in.astype(jnp.bfloat16), _row(b_in), w_dw, _row(b_dw), _row(ln_g), _row(ln_b),
      w_out.astype(jnp.bfloat16), _row(b_out))


def _route_kernel(x_ref, g_ref, wr_ref, hp_ref, cls_ref, rank_ref, cnt_ref, carry_ref):
    tr = x_ref.shape[0]

    @pl.when(pl.program_id(0) == 0)
    def _():
        carry_ref[...] = jnp.zeros_like(carry_ref)

    h = _rms(x_ref[...], g_ref[...])
    lt = lax.dot_general(wr_ref[...], h, (((1,), (1,)), ((), ())), precision=lax.Precision.HIGHEST,
                         preferred_element_type=jnp.float32)
    lg = [lt[i:i + 1, :] for i in range(N_EXPERT_GROUPS)]
    gmax = functools.reduce(jnp.maximum, lg)
    g_idx = jnp.full_like(gmax, N_EXPERT_GROUPS - 1, dtype=jnp.int32)
    for i in reversed(range(N_EXPERT_GROUPS - 1)):
        g_idx = jnp.where(lg[i] == gmax, i, g_idx)
    p_top = 1.0 / functools.reduce(jnp.add, [jnp.exp(v - gmax) for v in lg])
    sel = []
    for e in range(EXPERTS_PER_GROUP):
        v = lt[N_EXPERT_GROUPS + e:N_EXPERT_GROUPS + e + 1, :]
        for g in range(1, N_EXPERT_GROUPS):
            r = N_EXPERT_GROUPS + g * EXPERTS_PER_GROUP + e
            v = jnp.where(g_idx == g, lt[r:r + 1, :], v)
        sel.append(v)
    v1 = functools.reduce(jnp.maximum, sel)
    i1 = jnp.full_like(g_idx, EXPERTS_PER_GROUP - 1)
    for e in reversed(range(EXPERTS_PER_GROUP - 1)):
        i1 = jnp.where(sel[e] == v1, e, i1)
    rest = [jnp.where(i1 == e, -jnp.inf, sel[e]) for e in range(EXPERTS_PER_GROUP)]
    v2 = functools.reduce(jnp.maximum, rest)
    i2 = jnp.full_like(g_idx, EXPERTS_PER_GROUP - 1)
    for e in reversed(range(EXPERTS_PER_GROUP - 1)):
        i2 = jnp.where((rest[e] == v2) & (i1 != e), e, i2)
    i2 = jnp.where((i2 == i1), jnp.where(i1 == EXPERTS_PER_GROUP - 1, EXPERTS_PER_GROUP - 2, i2), i2)
    t2 = jnp.exp(v2 - v1)
    w1 = p_top / (1.0 + t2)
    w2 = p_top * t2 / (1.0 + t2)
    first_lo = i1 < i2
    e_lo = jnp.where(first_lo, i1, i2)
    e_hi = jnp.where(first_lo, i2, i1)
    w_lo = jnp.where(first_lo, w1, w2)
    w_hi = jnp.where(first_lo, w2, w1)
    pair_base = jnp.where(e_lo == 0, 0, jnp.where(e_lo == 1, EXPERTS_PER_GROUP - 1, 2 * EXPERTS_PER_GROUP - 3))
    cls = g_idx * N_PAIRS + pair_base + (e_hi - e_lo - 1)

    onehot = (lax.broadcasted_iota(jnp.int32, (CLASS_ROWS, tr), 0) == cls)
    before = (lax.broadcasted_iota(jnp.int32, (tr, tr), 0) < lax.broadcasted_iota(jnp.int32, (tr, tr), 1))
    cum = jnp.dot(onehot.astype(jnp.bfloat16), before.astype(jnp.bfloat16), preferred_element_type=jnp.float32)
    oh = onehot.astype(jnp.float32)
    rank = jnp.sum(oh * (cum + carry_ref[...]), axis=0, keepdims=True)
    carry_ref[...] = carry_ref[...] + jnp.sum(oh, axis=1, keepdims=True)
    cls_ref[0] = cls
    rank_ref[0] = rank.astype(jnp.int32)
    cnt_ref[...] = jnp.broadcast_to(carry_ref[...], cnt_ref.shape)

    half = D_MODEL // 2
    word = _pack_pair(h[:, :half], h[:, half:])
    n_word = half // V7X_LANES
    for s in range(n_word):
        hp_ref[pl.ds(s, tr, stride=ROW_SLABS), :] = word[:, s * V7X_LANES:(s + 1) * V7X_LANES]
    wcols = jnp.concatenate([w_lo, w_hi, jnp.zeros((V7X_LANES - 2, tr), jnp.float32)], axis=0).T
    hp_ref[pl.ds(n_word, tr, stride=ROW_SLABS), :] = lax.bitcast_convert_type(wcols, jnp.uint32)
    for s in range(n_word + 1, ROW_SLABS):
        hp_ref[pl.ds(s, tr, stride=ROW_SLABS), :] = jnp.zeros((tr, V7X_LANES), jnp.uint32)


def _moe_route(x2, g, r_grp, r_exp):
    t, d = x2.shape
    nt = t // SEQ_TILE
    wr = jnp.concatenate([r_grp, jnp.transpose(r_exp, (1, 0, 2)).reshape(d, N_EXPERTS)], axis=1)
    wr = jnp.pad(wr, ((0, 0), (0, CLASS_ROWS - wr.shape[1]))).T
    vec = lambda dt: jax.ShapeDtypeStruct((nt, 1, SEQ_TILE), dt)
    vec_spec = pl.BlockSpec((1, 1, SEQ_TILE), lambda i: (i, 0, 0))
    return pl.pallas_call(
        _route_kernel,
        grid=(nt,),
        in_specs=[pl.BlockSpec((SEQ_TILE, d), lambda i: (i, 0)),
                  pl.BlockSpec((1, d), lambda i: (0, 0)),
                  pl.BlockSpec((CLASS_ROWS, d), lambda i: (0, 0))],
        out_specs=[pl.BlockSpec((SEQ_TILE * ROW_SLABS, V7X_LANES), lambda i: (i, 0)), vec_spec, vec_spec,
                   pl.BlockSpec((CLASS_ROWS, V7X_LANES), lambda i: (0, 0))],
        out_shape=[jax.ShapeDtypeStruct((t * ROW_SLABS, V7X_LANES), jnp.uint32), vec(jnp.int32), vec(jnp.int32),
                   jax.ShapeDtypeStruct((CLASS_ROWS, V7X_LANES), jnp.float32)],
        scratch_shapes=[pltpu.VMEM((CLASS_ROWS, 1), jnp.float32)],
        compiler_params=_cparams("arbitrary"),
        name="moe_route",
    )(x2, _row(g), wr)


def _row_slice(ref, row):
    return ref.at[pl.ds(pl.multiple_of(row * ROW_SLABS, ROW_SLABS), ROW_SLABS)]


def _scatter_kernel(pos_ref, src_ref, _, dst_ref, sem):
    n = pos_ref.shape[2]

    def issue(k, carry):
        pltpu.make_async_copy(_row_slice(src_ref, k), _row_slice(dst_ref, pos_ref[0, 0, k]), sem).start()
        return carry

    lax.fori_loop(0, n, issue, 0)
    pltpu.make_async_copy(src_ref, dst_ref.at[pl.ds(0, n * ROW_SLABS)], sem).wait()


def _row_scatter(src, pos3, dst_init):
    nt, _, n = pos3.shape
    any_spec = pl.BlockSpec(memory_space=pl.ANY)
    return pl.pallas_call(
        _scatter_kernel,
        grid=(nt,),
        in_specs=[pl.BlockSpec((1, 1, n), lambda i: (i, 0, 0), memory_space=pltpu.SMEM),
                  pl.BlockSpec((n * ROW_SLABS, V7X_LANES), lambda i: (i, 0)), any_spec],
        out_specs=any_spec,
        out_shape=jax.ShapeDtypeStruct(dst_init.shape, dst_init.dtype),
        scratch_shapes=[pltpu.SemaphoreType.DMA],
        input_output_aliases={2: 0},
        compiler_params=_cparams("arbitrary"),
        name="row_scatter",
    )(pos3, src, dst_init)


def _gather_start(ys_ref, pos_ref, buf_ref, sem, slot):
    def issue(k, carry):
        pltpu.make_async_copy(_row_slice(ys_ref, pos_ref[0, 0, k]), _row_slice(buf_ref.at[slot], k),
                              sem.at[slot]).start()
        return carry

    lax.fori_loop(0, pos_ref.shape[2], issue, 0)


def _gathered_rows(step, n_steps, ys_ref, pos_ref, pos_next_ref, buf_ref, sem):
    slot = step % 2

    @pl.when(step == 0)
    def _():
        _gather_start(ys_ref, pos_ref, buf_ref, sem, 0)

    @pl.when(step + 1 < n_steps)
    def _():
        _gather_start(ys_ref, pos_next_ref, buf_ref, sem, 1 - slot)

    n = buf_ref.shape[1]
    pltpu.make_async_copy(ys_ref.at[pl.ds(0, n)], buf_ref.at[slot], sem.at[slot]).wait()
    return _load_row8(buf_ref.at[slot], n // ROW_SLABS)


def _gather_operands(moe, n_steps, step_of):
    ys, pos3 = moe
    n = pos3.shape[2]
    smem = lambda m: pl.BlockSpec((1, 1, n), m, memory_space=pltpu.SMEM)
    cur = lambda *g: (step_of(*g), 0, 0)
    nxt = lambda *g: (jnp.minimum(step_of(*g) + 1, n_steps - 1), 0, 0)
    return ([ys, pos3, pos3], [pl.BlockSpec(memory_space=pl.ANY), smem(cur), smem(nxt)],
            [pltpu.VMEM((2, n * ROW_SLABS, V7X_LANES), jnp.float32), pltpu.SemaphoreType.DMA((2,))])


def _expert_kernel(lo_ref, hi_ref, nt_ref, hs_ref, wg_lo, wu_lo, wd_lo, wg_hi, wu_hi, wd_hi, y_ref):
    tm = EXPERT_TILE
    j = pl.program_id(0)

    @pl.when(j < nt_ref[0])
    def _():
        n_word = D_MODEL // 2 // V7X_LANES
        words = [hs_ref[pl.ds(s, tm, stride=ROW_SLABS), :] for s in range(n_word)]
        h = jnp.concatenate([_unpack_lo(w) for w in words] + [_unpack_hi(w) for w in words], axis=1)
        h = h.astype(jnp.bfloat16)
        gates = lax.bitcast_convert_type(hs_ref[pl.ds(n_word, tm, stride=ROW_SLABS), :], jnp.float32)
        y = None
        for col, (wg, wu, wd) in enumerate(((wg_lo, wu_lo, wd_lo), (wg_hi, wu_hi, wd_hi))):
            a = jnp.dot(h, wg[0], preferred_element_type=jnp.float32)
            u = jnp.dot(h, wu[0], preferred_element_type=jnp.float32)
            hid = a * jax.nn.sigmoid(a) * u * gates[:, col:col + 1]
            part = jnp.dot(hid.astype(jnp.bfloat16), wd[0], preferred_element_type=jnp.float32)
            y = part if y is None else y + part
        _store_row8(y_ref, y, tm)

    @pl.when(j >= nt_ref[0])
    def _():
        y_ref[...] = jnp.zeros_like(y_ref)


def _moe_experts(hs, tile_lo, tile_hi, n_tiles, w_gate, w_up, w_down):
    n_rows = hs.shape[0] // ROW_SLABS
    nt_max = n_rows // EXPERT_TILE
    d, f = w_gate.shape[1], w_gate.shape[2]
    act = lambda j, lo, hi, nt: (jnp.minimum(j, nt[0] - 1), 0)
    w_lo = lambda j, lo, hi, nt: (lo[j], 0, 0)
    w_hi = lambda j, lo, hi, nt: (hi[j], 0, 0)
    up_spec = lambda m: pl.BlockSpec((1, d, f), m)
    dn_spec = lambda m: pl.BlockSpec((1, f, d), m)
    blk = (EXPERT_TILE * ROW_SLABS, V7X_LANES)
    wg, wu, wd = (w.astype(jnp.bfloat16) for w in (w_gate, w_up, w_down))
    return pl.pallas_call(
        _expert_kernel,
        grid_spec=pltpu.PrefetchScalarGridSpec(
            num_scalar_prefetch=3,
            grid=(nt_max,),
            in_specs=[pl.BlockSpec(blk, act), up_spec(w_lo), up_spec(w_lo), dn_spec(w_lo),
                      up_spec(w_hi), up_spec(w_hi), dn_spec(w_hi)],
            out_specs=pl.BlockSpec(blk, lambda j, lo, hi, nt: (j, 0)),
        ),
        out_shape=jax.ShapeDtypeStruct((n_rows * ROW_SLABS, V7X_LANES), jnp.float32),
        compiler_params=_cparams("arbitrary"),
        name="moe_experts",
    )(tile_lo, tile_hi, n_tiles, hs, wg, wu, wd, wg, wu, wd)


def _moe(x2, g, r_grp, r_exp, w_gate, w_up, w_down):
    t = x2.shape[0]
    hp, cls, rank, cnt = _moe_route(x2, g, r_grp, r_exp)
    counts = cnt[:N_CLASSES, 0].astype(jnp.int32)
    tiles_per = (counts + EXPERT_TILE - 1) // EXPERT_TILE
    tile_end = jnp.cumsum(tiles_per)
    base = (tile_end - tiles_per) * EXPERT_TILE
    cls = cls.reshape(t)
    pos = jnp.sum(jnp.where(cls[:, None] == jnp.arange(N_CLASSES)[None, :], base[None, :], 0), axis=1)
    pos = (pos + rank.reshape(t)).astype(jnp.int32)
    nt_max = t // EXPERT_TILE + N_CLASSES
    tile_cls = jnp.sum(jnp.arange(nt_max)[:, None] >= tile_end[None, :], axis=1)
    tile_cls = jnp.minimum(tile_cls, N_CLASSES - 1)
    lo_tab = jnp.array([p[0] for p in _PAIRS], jnp.int32)
    hi_tab = jnp.array([p[1] for p in _PAIRS], jnp.int32)
    tile_g = tile_cls // N_PAIRS
    tile_lo = (tile_g * EXPERTS_PER_GROUP + lo_tab[tile_cls % N_PAIRS]).astype(jnp.int32)
    tile_hi = (tile_g * EXPERTS_PER_GROUP + hi_tab[tile_cls % N_PAIRS]).astype(jnp.int32)
    n_tiles = tile_end[-1:].astype(jnp.int32)
    n_sorted = nt_max * EXPERT_TILE
    pos3 = pos.reshape(t // SEQ_TILE, 1, SEQ_TILE)
    hs = _row_scatter(hp, pos3, jnp.zeros((n_sorted * ROW_SLABS, V7X_LANES), jnp.uint32))
    ys = _moe_experts(hs, tile_lo, tile_hi, n_tiles, w_gate, w_up, w_down)
    return ys, pos3


def _rope_tables(seq):
    pos = jnp.arange(seq, dtype=jnp.float32)
    inv_freq = ROPE_THETA ** (-jnp.arange(0, HEAD_DIM, 2, dtype=jnp.float32) / HEAD_DIM)
    ang = pos[:, None] * inv_freq[None, :]
    reps = V7X_LANES // (HEAD_DIM // 2)
    cos = jnp.tile(jnp.cos(ang), (1, reps))
    sign = jnp.tile(jnp.concatenate([-jnp.ones(HEAD_DIM // 2), jnp.ones(HEAD_DIM // 2)]), V7X_LANES // HEAD_DIM)
    sin = jnp.tile(jnp.sin(ang), (1, reps)) * sign[None, :]
    return cos, sin


def _rope_slab(v, cos, sin, lane):
    half = HEAD_DIM // 2
    partner = jnp.where((lane % HEAD_DIM) < half, pltpu.roll(v, V7X_LANES - half, axis=1), pltpu.roll(v, half, axis=1))
    return v * cos + partner * sin


def _proj_kernel(has_y, specs, *refs):
    n = len(specs)
    if has_y:
        x_ref, ys_ref, pos_ref, pos_next_ref, *refs = refs
    else:
        x_ref, *refs = refs
    cos_ref, sin_ref, *refs = refs
    g_refs, w_refs, refs = refs[:n], refs[n:2 * n], refs[2 * n:]
    if has_y:
        xo_ref, *refs = refs
    o_refs, gather_scratch = refs[:n], refs[n:]
    ts = x_ref.shape[0]
    x = x_ref[...]
    if has_y:
        x = x + _gathered_rows(pl.program_id(0), pl.num_programs(0), ys_ref, pos_ref, pos_next_ref, *gather_scratch)
        xo_ref[...] = x
    inv = lax.rsqrt(jnp.mean(x * x, axis=-1, keepdims=True) + EPS)
    xn = x * inv
    cos, sin = cos_ref[...], sin_ref[...]
    lane = lax.broadcasted_iota(jnp.int32, (ts, V7X_LANES), 1)
    for i, (use_rope, scale) in enumerate(specs):
        h = (xn * g_refs[i][...]).astype(jnp.bfloat16)
        for grp in range(N_DGROUPS):
            acc = jnp.dot(h, w_refs[i][:, grp * D_MODEL:(grp + 1) * D_MODEL], preferred_element_type=jnp.float32)
            slabs = []
            for s in range(ROW_SLABS):
                v = acc[:, s * V7X_LANES:(s + 1) * V7X_LANES]
                if use_rope:
                    v = _rope_slab(v, cos, sin, lane)
                if scale != 1.0:
                    v = v * scale
                slabs.append(v)
            for q in range(ROW_SLABS // 2):
                o_refs[i][grp, :, q * V7X_LANES:(q + 1) * V7X_LANES] = _pack_pair(slabs[2 * q], slabs[2 * q + 1])


def _qkv_proj(x2, moe, seq, projections):
    t, d = x2.shape
    ns = seq // SEQ_TILE
    n = len(projections)
    cos, sin = _rope_tables(seq)
    tok = pl.BlockSpec((SEQ_TILE, d), lambda i: (i, 0))
    rope_spec = pl.BlockSpec((SEQ_TILE, V7X_LANES), lambda i: (i % ns, 0))
    acts, act_specs, gather_scratch = [x2], [tok], []
    if moe is not None:
        g_args, g_specs, gather_scratch = _gather_operands(moe, t // SEQ_TILE, lambda i: i)
        acts += g_args
        act_specs += g_specs
    packed = jax.ShapeDtypeStruct((N_DGROUPS, t, d // 2), jnp.uint32)
    packed_spec = pl.BlockSpec((N_DGROUPS, SEQ_TILE, d // 2), lambda i: (0, i, 0))
    out_shape, out_specs = [packed] * n, [packed_spec] * n
    if moe is not None:
        out_shape = [jax.ShapeDtypeStruct((t, d), jnp.float32)] + out_shape
        out_specs = [tok] + out_specs
    specs = tuple((p[2], p[3]) for p in projections)
    return pl.pallas_call(
        functools.partial(_proj_kernel, moe is not None, specs),
        grid=(t // SEQ_TILE,),
        in_specs=act_specs + [rope_spec, rope_spec]
                 + [pl.BlockSpec((1, d), lambda i: (0, 0))] * n
                 + [pl.BlockSpec((d, N_DGROUPS * d), lambda i: (0, 0), pipeline_mode=pl.Buffered(1))] * n,
        out_specs=out_specs,
        out_shape=out_shape,
        scratch_shapes=gather_scratch,
        compiler_params=_cparams("arbitrary"),
        name="qkv_proj",
    )(*acts, cos, sin, *[_row(p[0]) for p in projections], *[p[1].astype(jnp.bfloat16) for p in projections])


def _attn_unit(pair, qw, kw, vw, keep, lane):
    unpack = _unpack_lo if pair == 0 else _unpack_hi
    q = unpack(qw).astype(jnp.bfloat16)
    k = unpack(kw).astype(jnp.bfloat16)
    v = unpack(vw).astype(jnp.bfloat16)
    first = lane < HEAD_DIM
    zero = jnp.zeros_like(q)
    qs = jnp.concatenate([jnp.where(first, q, zero), jnp.where(first, zero, q)], axis=0)
    s = lax.dot_general(qs, k, (((1,), (1,)), ((), ())), preferred_element_type=jnp.float32)
    s = jnp.where(keep, s, NEG_BIG)
    m = jnp.max(s, axis=1, keepdims=True)
    p = jnp.exp(s - m)
    l = jnp.sum(p, axis=1, keepdims=True)
    pv = jnp.dot(p.astype(jnp.bfloat16), v, preferred_element_type=jnp.float32)
    nq = ATT_BLOCK
    o = jnp.where(first, pv[:nq], pv[nq:])
    shape = (nq, V7X_LANES)
    m2 = jnp.where(first, jnp.broadcast_to(m[:nq], shape), jnp.broadcast_to(m[nq:], shape))
    l2 = jnp.where(first, jnp.broadcast_to(l[:nq], shape), jnp.broadcast_to(l[nq:], shape))
    return o, m2, l2


def _attn_kernel(q_ref, k_ref, v_ref, o_ref, acc_ref, m_ref, l_ref):
    seq = q_ref.shape[1]
    nq = ATT_BLOCK
    lane = lax.broadcasted_iota(jnp.int32, (nq, V7X_LANES), 1)
    qi = lax.broadcasted_iota(jnp.int32, (2 * nq, 2 * nq), 0) % nq
    kj = lax.broadcasted_iota(jnp.int32, (2 * nq, 2 * nq), 1)
    band_prev = (kj < nq) & (kj >= qi)
    band_cur = (kj >= nq) & (kj - nq <= qi)
    causal = band_cur[:, nq:]

    for grp, (_, dil) in enumerate(DILATED_GROUPS):
        nblk = seq // dil // nq

        def rows(start):
            return pl.ds(pl.multiple_of(start, nq), nq) if dil == 1 else pl.ds(start, nq, stride=dil)

        def unit(idx):
            r, n = idx // nblk, idx % nblk
            start = n * nq * dil + r
            qw = q_ref[grp, rows(start), :]
            kc, vc = k_ref[grp, rows(start), :], v_ref[grp, rows(start), :]
            if nblk > 1:
                prev = jnp.maximum(n - 1, 0) * nq * dil + r
                kw = jnp.concatenate([k_ref[grp, rows(prev), :], kc], axis=0)
                vw = jnp.concatenate([v_ref[grp, rows(prev), :], vc], axis=0)
                keep = band_cur | (band_prev & (n > 0))
            else:
                kw, vw, keep = kc, vc, causal
            for pair in range(2):
                o, m2, l2 = _attn_unit(pair, qw, kw, vw, keep, lane)
                if grp == 0:
                    acc_ref[pair, rows(start), :] = o
                    m_ref[pair, rows(start), :] = m2
                    l_ref[pair, rows(start), :] = l2
                else:
                    m_old = m_ref[pair, rows(start), :]
                    m_new = jnp.maximum(m_old, m2)
                    a_old, a_new = jnp.exp(m_old - m_new), jnp.exp(m2 - m_new)
                    acc_ref[pair, rows(start), :] = acc_ref[pair, rows(start), :] * a_old + o * a_new
                    l_ref[pair, rows(start), :] = l_ref[pair, rows(start), :] * a_old + l2 * a_new
                    m_ref[pair, rows(start), :] = m_new

        def units(it, carry):
            for u in range(ATT_UNROLL):
                unit(it * ATT_UNROLL + u)
            return carry

        lax.fori_loop(0, seq // nq // ATT_UNROLL, units, 0)

    for pair in range(2):
        o_ref[:, pair * V7X_LANES:(pair + 1) * V7X_LANES] = (acc_ref[pair] / l_ref[pair]).astype(o_ref.dtype)


def _dilated_attention(qp, kp, vp, batch):
    t = qp.shape[1]
    seq = t // batch
    nquad = qp.shape[2] // V7X_LANES
    spec = pl.BlockSpec((N_DGROUPS, seq, V7X_LANES), lambda b, c: (0, b, c))
    return pl.pallas_call(
        _attn_kernel,
        grid=(batch, nquad),
        in_specs=[spec, spec, spec],
        out_specs=pl.BlockSpec((seq, 2 * V7X_LANES), lambda b, c: (b, c)),
        out_shape=jax.ShapeDtypeStruct((t, D_MODEL), jnp.bfloat16),
        scratch_shapes=[pltpu.VMEM((2, seq, V7X_LANES), jnp.float32)] * 3,
        compiler_params=_cparams("arbitrary", "arbitrary"),
        name="dilated_attention",
    )(qp, kp, vp)


def _attn_out_kernel(x_ref, c_ref, w_ref, o_ref):
    o_ref[...] = x_ref[...] + jnp.dot(c_ref[...], w_ref[...], preferred_element_type=jnp.float32)


def _attn_out(x2, comb, w_o):
    t, d = x2.shape
    tok = pl.BlockSpec((SEQ_TILE, d), lambda i: (i, 0))
    return pl.pallas_call(
        _attn_out_kernel,
        grid=(t // SEQ_TILE,),
        in_specs=[tok, tok, pl.BlockSpec((d, d), lambda i: (0, 0))],
        out_specs=tok,
        out_shape=jax.ShapeDtypeStruct((t, d), jnp.float32),
        compiler_params=_cparams("arbitrary"),
        name="attn_out",
    )(x2, comb, w_o.astype(jnp.bfloat16))


def _final_kernel(x_ref, ys_ref, pos_ref, pos_next_ref, g_ref, o_ref, *gather_scratch):
    y = _gathered_rows(pl.program_id(0), pl.num_programs(0), ys_ref, pos_ref, pos_next_ref, *gather_scratch)
    o_ref[...] = _rms(x_ref[...] + y, g_ref[...])


def _final_norm(x2, moe, g):
    t, d = x2.shape
    tok = pl.BlockSpec((SEQ_TILE, d), lambda i: (i, 0))
    g_args, g_specs, gather_scratch = _gather_operands(moe, t // SEQ_TILE, lambda i: i)
    return pl.pallas_call(
        _final_kernel,
        grid=(t // SEQ_TILE,),
        in_specs=[tok] + g_specs + [pl.BlockSpec((1, d), lambda i: (0, 0))],
        out_specs=tok,
        out_shape=jax.ShapeDtypeStruct((t, d), jnp.float32),
        scratch_shapes=gather_scratch,
        compiler_params=_cparams("arbitrary"),
        name="final_norm",
    )(x2, *g_args, _row(g))


def kernel(x, a_norm, a_w_in, a_b_in, a_w_dw, a_b_dw, a_ln_g, a_ln_b, a_w_out, a_b_out, kv_norm, w_kv, b_norm,
           b_w_q, b_w_o, ffn_norm, router_group, router_expert, w_gate, w_up, w_down, final_norm):
    batch, seq, d = x.shape
    n_a = a_norm.shape[0]
    depth = ffn_norm.shape[0]
    gw = N_DGROUPS * d
    q_scale = HEAD_DIM ** -0.5
    x2 = x.reshape(batch * seq, d)
    moe = None
    kp = vp = None
    for l in range(depth):
        if l < n_a:
            x2 = _a_layer(x2, moe, batch, a_norm[l], a_w_in[l], a_b_in[l], a_w_dw[l], a_b_dw[l], a_ln_g[l],
                          a_ln_b[l], a_w_out[l], a_b_out[l])
        else:
            j = l - n_a
            projections = [(b_norm[j], b_w_q[j], True, q_scale)]
            if kp is None:
                projections += [(kv_norm, w_kv[:, :gw], True, 1.0), (kv_norm, w_kv[:, gw:], False, 1.0)]
            outs = _qkv_proj(x2, moe, seq, projections)
            if moe is not None:
                x2, *outs = outs
            if kp is None:
                qp, kp, vp = outs
            else:
                qp, = outs
            comb = _dilated_attention(qp, kp, vp, batch)
            x2 = _attn_out(x2, comb, b_w_o[j])
        moe = _moe(x2, ffn_norm[l], router_group[l], router_expert[l], w_gate[l], w_up[l], w_down[l])
    return _final_norm(x2, moe, final_norm).reshape(batch, seq, d)
```

```python
import functools
import math

import jax
import jax.numpy as jnp
import numpy as np
from jax import lax
from jax.experimental import pallas as pl
from jax.experimental.pallas import tpu as pltpu

D_MODEL = 1024
CONV_KERNEL = 31
HEAD_DIM = 64
N_HEADS = 16
DILATED_GROUPS = ((128, 1), (512, 4), (2048, 16))
N_DGROUPS = len(DILATED_GROUPS)
ATT_BLOCK = 128
ROPE_THETA = 10000.0
N_EXPERT_GROUPS = 4
EXPERTS_PER_GROUP = 4
N_EXPERTS = N_EXPERT_GROUPS * EXPERTS_PER_GROUP
D_EXPERT = D_MODEL // 2
EPS = 1e-6

V7X_LANES = 128
V7X_SUBLANES = 8
V7X_VMEM_LIMIT_BYTES = 56 * 1024 * 1024

SEQ_TILE = 512
CONV_HALO = 32
CONV_CHUNK = 32
ROW_SLABS = D_MODEL // V7X_LANES
EXPERT_TILE = 256
ATT_UNROLL = 4
DMA_THREADS = 2
LOG2E = math.log2(math.e)
N_PAIRS = EXPERTS_PER_GROUP * (EXPERTS_PER_GROUP - 1) // 2
N_CLASSES = N_EXPERT_GROUPS * N_PAIRS
CLASS_ROWS = 32
NEG_BIG = -1e30

_PAIRS = [(a, b) for a in range(EXPERTS_PER_GROUP) for b in range(a + 1, EXPERTS_PER_GROUP)]


def _cparams(*sem):
    return pltpu.CompilerParams(dimension_semantics=sem, vmem_limit_bytes=V7X_VMEM_LIMIT_BYTES)


def _rms(x, g):
    return x * lax.rsqrt(jnp.mean(x * x, axis=-1, keepdims=True) + EPS) * g


def _load_row8(ref, n):
    return jnp.concatenate([ref[pl.ds(s, n, stride=ROW_SLABS), :] for s in range(ROW_SLABS)], axis=1)


def _store_row8(ref, val, n):
    for s in range(ROW_SLABS):
        ref[pl.ds(s, n, stride=ROW_SLABS), :] = val[:, s * V7X_LANES:(s + 1) * V7X_LANES]


def _bf16_bits(x):
    return lax.bitcast_convert_type(x.astype(jnp.bfloat16).astype(jnp.float32), jnp.uint32)


def _pack_pair(lo, hi):
    return (_bf16_bits(lo) >> 16) | _bf16_bits(hi)


def _unpack_lo(w):
    return lax.bitcast_convert_type(w << 16, jnp.float32)


def _unpack_hi(w):
    return lax.bitcast_convert_type(w & jnp.uint32(0xFFFF0000), jnp.float32)


def _a_layer_kernel(has_y, *refs):
    if has_y:
        x_ref, ys_ref, pos_ref, pos_next_ref, *refs = refs
    else:
        x_ref, *refs = refs
    (g_ref, win_ref, bin_ref, wdw_ref, bdw_ref, lng_ref, lnb_ref, wout_ref, bout_ref, o_ref,
     ext_ref, sh_ref, conv_ref, *gather_scratch) = refs
    ts, d = x_ref.shape

    @pl.when(pl.program_id(1) == 0)
    def _():
        ext_ref[0:CONV_HALO, :] = jnp.zeros((CONV_HALO, d), jnp.float32)

    x = x_ref[...]
    if has_y:
        step = pl.program_id(0) * pl.num_programs(1) + pl.program_id(1)
        n_steps = pl.num_programs(0) * pl.num_programs(1)
        x = x + _gathered_rows(step, n_steps, ys_ref, pos_ref, pos_next_ref, *gather_scratch)
    h = _rms(x, g_ref[...])
    u = jnp.dot(h.astype(jnp.bfloat16), win_ref[...], preferred_element_type=jnp.float32) + bin_ref[...]
    ext_ref[CONV_HALO:CONV_HALO + ts, :] = u[:, :d] * jax.nn.sigmoid(u[:, d:])

    base = CONV_HALO - (CONV_KERNEL - 1)
    nsh = ts + CONV_HALO - V7X_SUBLANES
    for j in range(1, V7X_SUBLANES):
        sh_ref[j - 1] = ext_ref[pl.ds(j, nsh), :]

    def chunk(c, carry):
        r = pl.multiple_of(c * CONV_CHUNK, CONV_CHUNK)
        acc = jnp.broadcast_to(bdw_ref[...], (CONV_CHUNK, d))
        for k in range(CONV_KERNEL):
            q, j = divmod(base + k, V7X_SUBLANES)
            rows = pl.ds(r + q * V7X_SUBLANES, CONV_CHUNK)
            tap = ext_ref[rows, :] if j == 0 else sh_ref[j - 1, rows, :]
            acc = acc + wdw_ref[k:k + 1, :] * tap
        conv_ref[pl.ds(r, CONV_CHUNK), :] = acc
        return carry

    lax.fori_loop(0, ts // CONV_CHUNK, chunk, 0)
    ext_ref[0:CONV_HALO, :] = ext_ref[ts:ts + CONV_HALO, :]

    c = conv_ref[...]
    mu = jnp.mean(c, axis=-1, keepdims=True)
    cc = c - mu
    var = jnp.mean(cc * cc, axis=-1, keepdims=True)
    v = cc * lax.rsqrt(var + EPS) * lng_ref[...] + lnb_ref[...]
    v = v * jax.nn.sigmoid(v)
    out = jnp.dot(v.astype(jnp.bfloat16), wout_ref[...], preferred_element_type=jnp.float32)
    o_ref[...] = x + out + bout_ref[...]


def _row(v):
    return v.reshape(1, -1)


def _a_layer(x2, moe, batch, g, w_in, b_in, w_dw, b_dw, ln_g, ln_b, w_out, b_out):
    t, d = x2.shape
    ns = t // batch // SEQ_TILE
    const = lambda shape: pl.BlockSpec(shape, lambda b, s: (0, 0))
    tok = pl.BlockSpec((SEQ_TILE, d), lambda b, s: (b * ns + s, 0))
    acts, act_specs, gather_scratch = [x2], [tok], []
    if moe is not None:
        g_args, g_specs, gather_scratch = _gather_operands(moe, batch * ns, lambda b, s: b * ns + s)
        acts += g_args
        act_specs += g_specs
    return pl.pallas_call(
        functools.partial(_a_layer_kernel, moe is not None),
        grid=(batch, ns),
        in_specs=act_specs + [
            const((1, d)), const((d, 2 * d)), const((1, 2 * d)), const((CONV_KERNEL, d)), const((1, d)),
            const((1, d)), const((1, d)), const((d, d)), const((1, d)),
        ],
        out_specs=tok,
        out_shape=jax.ShapeDtypeStruct((t, d), jnp.float32),
        scratch_shapes=[pltpu.VMEM((CONV_HALO + SEQ_TILE, d), jnp.float32),
                        pltpu.VMEM((V7X_SUBLANES - 1, CONV_HALO + SEQ_TILE - V7X_SUBLANES, d), jnp.float32),
                        pltpu.VMEM((SEQ_TILE, d), jnp.float32)] + gather_scratch,
        compiler_params=_cparams("arbitrary", "arbitrary"),
        name="a_layer",
    )(*acts, _row(g), w_in.astype(jnp.bfloat16), _row(b_in), w_dw, _row(b_dw), _row(ln_g), _row(ln_b),
      w_out.astype(jnp.bfloat16), _row(b_out))


def _route_kernel(x_ref, g_ref, wr_ref, hp_ref, cls_ref, rank_ref, cnt_ref, carry_ref):
    tr = x_ref.shape[0]

    @pl.when(pl.program_id(0) == 0)
    def _():
        carry_ref[...] = jnp.zeros_like(carry_ref)

    h = _rms(x_ref[...], g_ref[...])
    lt = lax.dot_general(wr_ref[...], h.astype(jnp.bfloat16), (((1,), (1,)), ((), ())),
                         preferred_element_type=jnp.float32)
    lg = [lt[i:i + 1, :] for i in range(N_EXPERT_GROUPS)]
    gmax = functools.reduce(jnp.maximum, lg)
    g_idx = jnp.full_like(gmax, N_EXPERT_GROUPS - 1, dtype=jnp.int32)
    for i in reversed(range(N_EXPERT_GROUPS - 1)):
        g_idx = jnp.where(lg[i] == gmax, i, g_idx)
    p_top = 1.0 / functools.reduce(jnp.add, [jnp.exp(v - gmax) for v in lg])
    sel = []
    for e in range(EXPERTS_PER_GROUP):
        v = lt[N_EXPERT_GROUPS + e:N_EXPERT_GROUPS + e + 1, :]
        for g in range(1, N_EXPERT_GROUPS):
            r = N_EXPERT_GROUPS + g * EXPERTS_PER_GROUP + e
            v = jnp.where(g_idx == g, lt[r:r + 1, :], v)
        sel.append(v)
    v1 = functools.reduce(jnp.maximum, sel)
    i1 = jnp.full_like(g_idx, EXPERTS_PER_GROUP - 1)
    for e in reversed(range(EXPERTS_PER_GROUP - 1)):
        i1 = jnp.where(sel[e] == v1, e, i1)
    rest = [jnp.where(i1 == e, -jnp.inf, sel[e]) for e in range(EXPERTS_PER_GROUP)]
    v2 = functools.reduce(jnp.maximum, rest)
    i2 = jnp.full_like(g_idx, EXPERTS_PER_GROUP - 1)
    for e in reversed(range(EXPERTS_PER_GROUP - 1)):
        i2 = jnp.where((rest[e] == v2) & (i1 != e), e, i2)
    i2 = jnp.where((i2 == i1), jnp.where(i1 == EXPERTS_PER_GROUP - 1, EXPERTS_PER_GROUP - 2, i2), i2)
    t2 = jnp.exp(v2 - v1)
    w1 = p_top / (1.0 + t2)
    w2 = p_top * t2 / (1.0 + t2)
    first_lo = i1 < i2
    e_lo = jnp.where(first_lo, i1, i2)
    e_hi = jnp.where(first_lo, i2, i1)
    w_lo = jnp.where(first_lo, w1, w2)
    w_hi = jnp.where(first_lo, w2, w1)
    pair_base = jnp.where(e_lo == 0, 0, jnp.where(e_lo == 1, EXPERTS_PER_GROUP - 1, 2 * EXPERTS_PER_GROUP - 3))
    cls = g_idx * N_PAIRS + pair_base + (e_hi - e_lo - 1)

    onehot = (lax.broadcasted_iota(jnp.int32, (CLASS_ROWS, tr), 0) == cls)
    before = (lax.broadcasted_iota(jnp.int32, (tr, tr), 0) < lax.broadcasted_iota(jnp.int32, (tr, tr), 1))
    cum = jnp.dot(onehot.astype(jnp.bfloat16), before.astype(jnp.bfloat16), preferred_element_type=jnp.float32)
    oh = onehot.astype(jnp.float32)
    rank = jnp.sum(oh * (cum + carry_ref[...]), axis=0, keepdims=True)
    carry_ref[...] = carry_ref[...] + jnp.sum(oh, axis=1, keepdims=True)
    cls_ref[0] = cls
    rank_ref[0] = rank.astype(jnp.int32)
    cnt_ref[...] = jnp.broadcast_to(carry_ref[...], cnt_ref.shape)

    half = D_MODEL // 2
    word = _pack_pair(h[:, :half], h[:, half:])
    n_word = half // V7X_LANES
    for s in range(n_word):
        hp_ref[pl.ds(s, tr, stride=ROW_SLABS), :] = word[:, s * V7X_LANES:(s + 1) * V7X_LANES]
    wcols = jnp.concatenate([w_lo, w_hi, jnp.zeros((V7X_LANES - 2, tr), jnp.float32)], axis=0).T
    hp_ref[pl.ds(n_word, tr, stride=ROW_SLABS), :] = lax.bitcast_convert_type(wcols, jnp.uint32)
    for s in range(n_word + 1, ROW_SLABS):
        hp_ref[pl.ds(s, tr, stride=ROW_SLABS), :] = jnp.zeros((tr, V7X_LANES), jnp.uint32)


def _moe_route(x2, g, r_grp, r_exp):
    t, d = x2.shape
    nt = t // SEQ_TILE
    wr = jnp.concatenate([r_grp, jnp.transpose(r_exp, (1, 0, 2)).reshape(d, N_EXPERTS)], axis=1)
    wr = jnp.pad(wr, ((0, 0), (0, CLASS_ROWS - wr.shape[1]))).T.astype(jnp.bfloat16)
    vec = lambda dt: jax.ShapeDtypeStruct((nt, 1, SEQ_TILE), dt)
    vec_spec = pl.BlockSpec((1, 1, SEQ_TILE), lambda i: (i, 0, 0))
    return pl.pallas_call(
        _route_kernel,
        grid=(nt,),
        in_specs=[pl.BlockSpec((SEQ_TILE, d), lambda i: (i, 0)),
                  pl.BlockSpec((1, d), lambda i: (0, 0)),
                  pl.BlockSpec((CLASS_ROWS, d), lambda i: (0, 0))],
        out_specs=[pl.BlockSpec((SEQ_TILE * ROW_SLABS, V7X_LANES), lambda i: (i, 0)), vec_spec, vec_spec,
                   pl.BlockSpec((CLASS_ROWS, V7X_LANES), lambda i: (0, 0))],
        out_shape=[jax.ShapeDtypeStruct((t * ROW_SLABS, V7X_LANES), jnp.uint32), vec(jnp.int32), vec(jnp.int32),
                   jax.ShapeDtypeStruct((CLASS_ROWS, V7X_LANES), jnp.float32)],
        scratch_shapes=[pltpu.VMEM((CLASS_ROWS, 1), jnp.float32)],
        compiler_params=_cparams("arbitrary"),
        name="moe_route",
    )(x2, _row(g), wr)


def _row_slice(ref, row):
    return ref.at[pl.ds(pl.multiple_of(row * ROW_SLABS, ROW_SLABS), ROW_SLABS)]


def _scatter_kernel(pos_ref, src_ref, _, dst_ref, sem):
    n = pos_ref.shape[2]

    def issue(i, carry):
        for u in range(DMA_THREADS):
            k = i * DMA_THREADS + u
            pltpu.make_async_copy(_row_slice(src_ref, k), _row_slice(dst_ref, pos_ref[0, 0, k]),
                                  sem).start(priority=u)
        return carry

    lax.fori_loop(0, n // DMA_THREADS, issue, 0)
    pltpu.make_async_copy(src_ref, dst_ref.at[pl.ds(0, n * ROW_SLABS)], sem).wait()


def _row_scatter(src, pos3, dst_init):
    nt, _, n = pos3.shape
    any_spec = pl.BlockSpec(memory_space=pl.ANY)
    return pl.pallas_call(
        _scatter_kernel,
        grid=(nt,),
        in_specs=[pl.BlockSpec((1, 1, n), lambda i: (i, 0, 0), memory_space=pltpu.SMEM),
                  pl.BlockSpec((n * ROW_SLABS, V7X_LANES), lambda i: (i, 0)), any_spec],
        out_specs=any_spec,
        out_shape=jax.ShapeDtypeStruct(dst_init.shape, dst_init.dtype),
        scratch_shapes=[pltpu.SemaphoreType.DMA],
        input_output_aliases={2: 0},
        compiler_params=_cparams("arbitrary"),
        name="row_scatter",
    )(pos3, src, dst_init)


def _gather_start(ys_ref, pos_ref, buf_ref, sem, slot):
    def issue(i, carry):
        for u in range(DMA_THREADS):
            k = i * DMA_THREADS + u
            pltpu.make_async_copy(_row_slice(ys_ref, pos_ref[0, 0, k]), _row_slice(buf_ref.at[slot], k),
                                  sem.at[slot]).start(priority=u)
        return carry

    lax.fori_loop(0, pos_ref.shape[2] // DMA_THREADS, issue, 0)


def _gathered_rows(step, n_steps, ys_ref, pos_ref, pos_next_ref, buf_ref, sem):
    slot = step % 2

    @pl.when(step == 0)
    def _():
        _gather_start(ys_ref, pos_ref, buf_ref, sem, 0)

    @pl.when(step + 1 < n_steps)
    def _():
        _gather_start(ys_ref, pos_next_ref, buf_ref, sem, 1 - slot)

    n = buf_ref.shape[1]
    pltpu.make_async_copy(ys_ref.at[pl.ds(0, n)], buf_ref.at[slot], sem.at[slot]).wait()
    return _load_row8(buf_ref.at[slot], n // ROW_SLABS)


def _gather_operands(moe, n_steps, step_of):
    ys, pos3 = moe
    n = pos3.shape[2]
    smem = lambda m: pl.BlockSpec((1, 1, n), m, memory_space=pltpu.SMEM)
    cur = lambda *g: (step_of(*g), 0, 0)
    nxt = lambda *g: (jnp.minimum(step_of(*g) + 1, n_steps - 1), 0, 0)
    return ([ys, pos3, pos3], [pl.BlockSpec(memory_space=pl.ANY), smem(cur), smem(nxt)],
            [pltpu.VMEM((2, n * ROW_SLABS, V7X_LANES), jnp.float32), pltpu.SemaphoreType.DMA((2,))])


def _expert_kernel(lo_ref, hi_ref, nt_ref, hs_ref, wg_lo, wu_lo, wd_lo, wg_hi, wu_hi, wd_hi, y_ref):
    tm = EXPERT_TILE
    j = pl.program_id(0)

    @pl.when(j < nt_ref[0])
    def _():
        n_word = D_MODEL // 2 // V7X_LANES
        words = [hs_ref[pl.ds(s, tm, stride=ROW_SLABS), :] for s in range(n_word)]
        h = jnp.concatenate([_unpack_lo(w) for w in words] + [_unpack_hi(w) for w in words], axis=1)
        h = h.astype(jnp.bfloat16)
        gates = lax.bitcast_convert_type(hs_ref[pl.ds(n_word, tm, stride=ROW_SLABS), :], jnp.float32)
        y = None
        for col, (wg, wu, wd) in enumerate(((wg_lo, wu_lo, wd_lo), (wg_hi, wu_hi, wd_hi))):
            a = jnp.dot(h, wg[0], preferred_element_type=jnp.float32)
            u = jnp.dot(h, wu[0], preferred_element_type=jnp.float32)
            hid = a * jax.nn.sigmoid(a) * u * gates[:, col:col + 1]
            part = jnp.dot(hid.astype(jnp.bfloat16), wd[0], preferred_element_type=jnp.float32)
            y = part if y is None else y + part
        _store_row8(y_ref, y, tm)

    @pl.when(j >= nt_ref[0])
    def _():
        y_ref[...] = jnp.zeros_like(y_ref)


def _moe_experts(hs, tile_lo, tile_hi, n_tiles, w_gate, w_up, w_down):
    n_rows = hs.shape[0] // ROW_SLABS
    nt_max = n_rows // EXPERT_TILE
    d, f = w_gate.shape[1], w_gate.shape[2]
    act = lambda j, lo, hi, nt: (jnp.maximum(jnp.minimum(j, nt[0] - 1), 0), 0)
    w_lo = lambda j, lo, hi, nt: (lo[j], 0, 0)
    w_hi = lambda j, lo, hi, nt: (hi[j], 0, 0)
    up_spec = lambda m: pl.BlockSpec((1, d, f), m)
    dn_spec = lambda m: pl.BlockSpec((1, f, d), m)
    blk = (EXPERT_TILE * ROW_SLABS, V7X_LANES)
    wg, wu, wd = (w.astype(jnp.bfloat16) for w in (w_gate, w_up, w_down))
    return pl.pallas_call(
        _expert_kernel,
        grid_spec=pltpu.PrefetchScalarGridSpec(
            num_scalar_prefetch=3,
            grid=(nt_max,),
            in_specs=[pl.BlockSpec(blk, act), up_spec(w_lo), up_spec(w_lo), dn_spec(w_lo),
                      up_spec(w_hi), up_spec(w_hi), dn_spec(w_hi)],
            out_specs=pl.BlockSpec(blk, lambda j, lo, hi, nt: (j, 0)),
        ),
        out_shape=jax.ShapeDtypeStruct((n_rows * ROW_SLABS, V7X_LANES), jnp.float32),
        compiler_params=_cparams("arbitrary"),
        name="moe_experts",
    )(tile_lo, tile_hi, n_tiles, hs, wg, wu, wd, wg, wu, wd)


def _moe(x2, g, r_grp, r_exp, w_gate, w_up, w_down):
    t = x2.shape[0]
    hp, cls, rank, cnt = _moe_route(x2, g, r_grp, r_exp)
    counts = cnt[:N_CLASSES, 0].astype(jnp.int32)
    tiles_per = (counts + EXPERT_TILE - 1) // EXPERT_TILE
    tile_end = jnp.cumsum(tiles_per)
    base = (tile_end - tiles_per) * EXPERT_TILE
    cls = cls.reshape(t)
    pos = jnp.sum(jnp.where(cls[:, None] == jnp.arange(N_CLASSES)[None, :], base[None, :], 0), axis=1)
    pos = (pos + rank.reshape(t)).astype(jnp.int32)
    nt_max = t // EXPERT_TILE + N_CLASSES
    tile_cls = jnp.sum(jnp.arange(nt_max)[:, None] >= tile_end[None, :], axis=1)
    tile_cls = jnp.minimum(tile_cls, N_CLASSES - 1)
    lo_tab = jnp.array([p[0] for p in _PAIRS], jnp.int32)
    hi_tab = jnp.array([p[1] for p in _PAIRS], jnp.int32)
    tile_g = tile_cls // N_PAIRS
    tile_lo = (tile_g * EXPERTS_PER_GROUP + lo_tab[tile_cls % N_PAIRS]).astype(jnp.int32)
    tile_hi = (tile_g * EXPERTS_PER_GROUP + hi_tab[tile_cls % N_PAIRS]).astype(jnp.int32)
    n_tiles = tile_end[-1:].astype(jnp.int32)
    n_sorted = nt_max * EXPERT_TILE
    pos3 = pos.reshape(t // SEQ_TILE, 1, SEQ_TILE)
    hs = _row_scatter(hp, pos3, jnp.zeros((n_sorted * ROW_SLABS, V7X_LANES), jnp.uint32))
    ys = _moe_experts(hs, tile_lo, tile_hi, n_tiles, w_gate, w_up, w_down)
    return ys, pos3


def _rope_tables(seq):
    pos = jnp.arange(seq, dtype=jnp.float32)
    inv_freq = ROPE_THETA ** (-jnp.arange(0, HEAD_DIM, 2, dtype=jnp.float32) / HEAD_DIM)
    ang = pos[:, None] * inv_freq[None, :]
    reps = V7X_LANES // (HEAD_DIM // 2)
    cos = jnp.tile(jnp.cos(ang), (1, reps))
    sign = jnp.tile(jnp.concatenate([-jnp.ones(HEAD_DIM // 2), jnp.ones(HEAD_DIM // 2)]), V7X_LANES // HEAD_DIM)
    sin = jnp.tile(jnp.sin(ang), (1, reps)) * sign[None, :]
    return cos, sin


def _rope_slab(v, cos, sin, lane):
    half = HEAD_DIM // 2
    partner = jnp.where((lane % HEAD_DIM) < half, pltpu.roll(v, V7X_LANES - half, axis=1), pltpu.roll(v, half, axis=1))
    return v * cos + partner * sin


def _proj_kernel(has_y, specs, *refs):
    n = len(specs)
    if has_y:
        x_ref, ys_ref, pos_ref, pos_next_ref, *refs = refs
    else:
        x_ref, *refs = refs
    cos_ref, sin_ref, *refs = refs
    g_refs, w_refs, refs = refs[:n], refs[n:2 * n], refs[2 * n:]
    if has_y:
        xo_ref, *refs = refs
    o_refs, gather_scratch = refs[:n], refs[n:]
    ts = x_ref.shape[0]
    x = x_ref[...]
    if has_y:
        x = x + _gathered_rows(pl.program_id(0), pl.num_programs(0), ys_ref, pos_ref, pos_next_ref, *gather_scratch)
        xo_ref[...] = x
    inv = lax.rsqrt(jnp.mean(x * x, axis=-1, keepdims=True) + EPS)
    xn = x * inv
    cos, sin = cos_ref[...], sin_ref[...]
    lane = lax.broadcasted_iota(jnp.int32, (ts, V7X_LANES), 1)
    for i, (use_rope, scale) in enumerate(specs):
        h = (xn * g_refs[i][...]).astype(jnp.bfloat16)
        for grp in range(N_DGROUPS):
            acc = jnp.dot(h, w_refs[i][:, grp * D_MODEL:(grp + 1) * D_MODEL], preferred_element_type=jnp.float32)
            slabs = []
            for s in range(ROW_SLABS):
                v = acc[:, s * V7X_LANES:(s + 1) * V7X_LANES]
                if use_rope:
                    v = _rope_slab(v, cos, sin, lane)
                if scale != 1.0:
                    v = v * scale
                slabs.append(v)
            for q in range(ROW_SLABS // 2):
                o_refs[i][grp, :, q * V7X_LANES:(q + 1) * V7X_LANES] = _pack_pair(slabs[2 * q], slabs[2 * q + 1])


def _qkv_proj(x2, moe, seq, projections):
    t, d = x2.shape
    ns = seq // SEQ_TILE
    n = len(projections)
    cos, sin = _rope_tables(seq)
    tok = pl.BlockSpec((SEQ_TILE, d), lambda i: (i, 0))
    rope_spec = pl.BlockSpec((SEQ_TILE, V7X_LANES), lambda i: (i % ns, 0))
    acts, act_specs, gather_scratch = [x2], [tok], []
    if moe is not None:
        g_args, g_specs, gather_scratch = _gather_operands(moe, t // SEQ_TILE, lambda i: i)
        acts += g_args
        act_specs += g_specs
    packed = jax.ShapeDtypeStruct((N_DGROUPS, t, d // 2), jnp.uint32)
    packed_spec = pl.BlockSpec((N_DGROUPS, SEQ_TILE, d // 2), lambda i: (0, i, 0))
    out_shape, out_specs = [packed] * n, [packed_spec] * n
    if moe is not None:
        out_shape = [jax.ShapeDtypeStruct((t, d), jnp.float32)] + out_shape
        out_specs = [tok] + out_specs
    specs = tuple((p[2], p[3]) for p in projections)
    return pl.pallas_call(
        functools.partial(_proj_kernel, moe is not None, specs),
        grid=(t // SEQ_TILE,),
        in_specs=act_specs + [rope_spec, rope_spec]
                 + [pl.BlockSpec((1, d), lambda i: (0, 0))] * n
                 + [pl.BlockSpec((d, N_DGROUPS * d), lambda i: (0, 0), pipeline_mode=pl.Buffered(1))] * n,
        out_specs=out_specs,
        out_shape=out_shape,
        scratch_shapes=gather_scratch,
        compiler_params=_cparams("arbitrary"),
        name="qkv_proj",
    )(*acts, cos, sin, *[_row(p[0]) for p in projections], *[p[1].astype(jnp.bfloat16) for p in projections])


def _attn_unit(pair, qw, kw, vw, bias, lane):
    unpack = _unpack_lo if pair == 0 else _unpack_hi
    q = unpack(qw).astype(jnp.bfloat16)
    k = unpack(kw).astype(jnp.bfloat16)
    v = unpack(vw).astype(jnp.bfloat16)
    first = lane < HEAD_DIM
    zero = jnp.zeros_like(q)
    qs = jnp.concatenate([jnp.where(first, q, zero), jnp.where(first, zero, q)], axis=0)
    s = lax.dot_general(qs, k, (((1,), (1,)), ((), ())), preferred_element_type=jnp.float32) + bias
    m = jnp.max(s, axis=1, keepdims=True)
    p = jnp.exp2(s - m)
    l = jnp.sum(p, axis=1, keepdims=True)
    pv = jnp.dot(p.astype(jnp.bfloat16), v, preferred_element_type=jnp.float32)
    nq = ATT_BLOCK
    o = jnp.where(first, pv[:nq], pv[nq:])
    shape = (nq, V7X_LANES)
    m2 = jnp.where(first, jnp.broadcast_to(m[:nq], shape), jnp.broadcast_to(m[nq:], shape))
    l2 = jnp.where(first, jnp.broadcast_to(l[:nq], shape), jnp.broadcast_to(l[nq:], shape))
    return o, m2, l2


def _attn_kernel(q_ref, k_ref, v_ref, o_ref, acc_ref, m_ref, l_ref):
    seq = q_ref.shape[1]
    nq = ATT_BLOCK
    lane = lax.broadcasted_iota(jnp.int32, (nq, V7X_LANES), 1)
    qi = lax.broadcasted_iota(jnp.int32, (2 * nq, 2 * nq), 0) % nq
    kj = lax.broadcasted_iota(jnp.int32, (2 * nq, 2 * nq), 1)
    band_prev = (kj < nq) & (kj >= qi)
    band_cur = (kj >= nq) & (kj - nq <= qi)
    bias_full = jnp.where(band_prev | band_cur, 0.0, NEG_BIG)
    bias_cur = bias_full[:, nq:]
    prev_cols = jnp.where(kj < nq, NEG_BIG, 0.0)

    for grp, (_, dil) in enumerate(DILATED_GROUPS):
        nblk = seq // dil // nq
        assert nblk % ATT_UNROLL == 0 or ATT_UNROLL % nblk == 0

        def rows(start):
            return pl.ds(pl.multiple_of(start, nq), nq) if dil == 1 else pl.ds(start, nq, stride=dil)

        def unit(it, u):
            idx = it * ATT_UNROLL + u
            n = u % nblk if ATT_UNROLL % nblk == 0 else idx % nblk
            r = idx // nblk
            start = n * nq * dil + r
            qw = q_ref[grp, rows(start), :]
            kw, vw = k_ref[grp, rows(start), :], v_ref[grp, rows(start), :]
            if isinstance(n, int) and n == 0:
                bias = bias_cur
            else:
                prev = jnp.maximum(n - 1, 0) * nq * dil + r
                kw = jnp.concatenate([k_ref[grp, rows(prev), :], kw], axis=0)
                vw = jnp.concatenate([v_ref[grp, rows(prev), :], vw], axis=0)
                bias = bias_full
                if not isinstance(n, int) and u == 0:
                    bias = bias + jnp.where(n > 0, 0.0, 1.0) * prev_cols
            for pair in range(2):
                o, m2, l2 = _attn_unit(pair, qw, kw, vw, bias, lane)
                if grp == 0:
                    acc_ref[pair, rows(start), :] = o
                    m_ref[pair, rows(start), :] = m2
                    l_ref[pair, rows(start), :] = l2
                else:
                    m_old = m_ref[pair, rows(start), :]
                    m_new = jnp.maximum(m_old, m2)
                    a_old, a_new = jnp.exp2(m_old - m_new), jnp.exp2(m2 - m_new)
                    acc_ref[pair, rows(start), :] = acc_ref[pair, rows(start), :] * a_old + o * a_new
                    l_ref[pair, rows(start), :] = l_ref[pair, rows(start), :] * a_old + l2 * a_new
                    m_ref[pair, rows(start), :] = m_new

        def units(it, carry):
            for u in range(ATT_UNROLL):
                unit(it, u)
            return carry

        lax.fori_loop(0, seq // nq // ATT_UNROLL, units, 0)

    for pair in range(2):
        o_ref[:, pair * V7X_LANES:(pair + 1) * V7X_LANES] = (acc_ref[pair] / l_ref[pair]).astype(o_ref.dtype)


def _dilated_attention(qp, kp, vp, batch):
    t = qp.shape[1]
    seq = t // batch
    nquad = qp.shape[2] // V7X_LANES
    spec = pl.BlockSpec((N_DGROUPS, seq, V7X_LANES), lambda b, c: (0, b, c))
    return pl.pallas_call(
        _attn_kernel,
        grid=(batch, nquad),
        in_specs=[spec, spec, spec],
        out_specs=pl.BlockSpec((seq, 2 * V7X_LANES), lambda b, c: (b, c)),
        out_shape=jax.ShapeDtypeStruct((t, D_MODEL), jnp.bfloat16),
        scratch_shapes=[pltpu.VMEM((2, seq, V7X_LANES), jnp.float32)] * 3,
        compiler_params=_cparams("arbitrary", "arbitrary"),
        name="dilated_attention",
    )(qp, kp, vp)


def _attn_out_kernel(x_ref, c_ref, w_ref, o_ref):
    o_ref[...] = x_ref[...] + jnp.dot(c_ref[...], w_ref[...], preferred_element_type=jnp.float32)


def _attn_out(x2, comb, w_o):
    t, d = x2.shape
    tok = pl.BlockSpec((SEQ_TILE, d), lambda i: (i, 0))
    return pl.pallas_call(
        _attn_out_kernel,
        grid=(t // SEQ_TILE,),
        in_specs=[tok, tok, pl.BlockSpec((d, d), lambda i: (0, 0))],
        out_specs=tok,
        out_shape=jax.ShapeDtypeStruct((t, d), jnp.float32),
        compiler_params=_cparams("arbitrary"),
        name="attn_out",
    )(x2, comb, w_o.astype(jnp.bfloat16))


def _final_kernel(x_ref, ys_ref, pos_ref, pos_next_ref, g_ref, o_ref, *gather_scratch):
    y = _gathered_rows(pl.program_id(0), pl.num_programs(0), ys_ref, pos_ref, pos_next_ref, *gather_scratch)
    o_ref[...] = _rms(x_ref[...] + y, g_ref[...])


def _final_norm(x2, moe, g):
    t, d = x2.shape
    tok = pl.BlockSpec((SEQ_TILE, d), lambda i: (i, 0))
    g_args, g_specs, gather_scratch = _gather_operands(moe, t // SEQ_TILE, lambda i: i)
    return pl.pallas_call(
        _final_kernel,
        grid=(t // SEQ_TILE,),
        in_specs=[tok] + g_specs + [pl.BlockSpec((1, d), lambda i: (0, 0))],
        out_specs=tok,
        out_shape=jax.ShapeDtypeStruct((t, d), jnp.float32),
        scratch_shapes=gather_scratch,
        compiler_params=_cparams("arbitrary"),
        name="final_norm",
    )(x2, *g_args, _row(g))


def kernel(x, a_norm, a_w_in, a_b_in, a_w_dw, a_b_dw, a_ln_g, a_ln_b, a_w_out, a_b_out, kv_norm, w_kv, b_norm,
           b_w_q, b_w_o, ffn_norm, router_group, router_expert, w_gate, w_up, w_down, final_norm):
    batch, seq, d = x.shape
    n_a = a_norm.shape[0]
    depth = ffn_norm.shape[0]
    gw = N_DGROUPS * d
    q_scale = HEAD_DIM ** -0.5 * LOG2E
    x2 = x.reshape(batch * seq, d)
    moe = None
    kp = vp = None
    for l in range(depth):
        if l < n_a:
            x2 = _a_layer(x2, moe, batch, a_norm[l], a_w_in[l], a_b_in[l], a_w_dw[l], a_b_dw[l], a_ln_g[l],
                          a_ln_b[l], a_w_out[l], a_b_out[l])
        else:
            j = l - n_a
            projections = [(b_norm[j], b_w_q[j], True, q_scale)]
            if kp is None:
                projections += [(kv_norm, w_kv[:, :gw], True, 1.0), (kv_norm, w_kv[:, gw:], False, 1.0)]
            outs = _qkv_proj(x2, moe, seq, projections)
            if moe is not None:
                x2, *outs = outs
            if kp is None:
                qp, kp, vp = outs
            else:
                qp, = outs
            comb = _dilated_attention(qp, kp, vp, batch)
            x2 = _attn_out(x2, comb, b_w_o[j])
        moe = _moe(x2, ffn_norm[l], router_group[l], router_expert[l], w_gate[l], w_up[l], w_down[l])
    return _final_norm(x2, moe, final_norm).reshape(batch, seq, d)
```

```python
import functools
import math

import jax
import jax.numpy as jnp
import numpy as np
from jax import lax
from jax.experimental import pallas as pl
from jax.experimental.pallas import tpu as pltpu

D_MODEL = 1024
CONV_KERNEL = 31
HEAD_DIM = 64
N_HEADS = 16
DILATED_GROUPS = ((128, 1), (512, 4), (2048, 16))
N_DGROUPS = len(DILATED_GROUPS)
ATT_BLOCK = 128
ROPE_THETA = 10000.0
N_EXPERT_GROUPS = 4
EXPERTS_PER_GROUP = 4
N_EXPERTS = N_EXPERT_GROUPS * EXPERTS_PER_GROUP
D_EXPERT = D_MODEL // 2
EPS = 1e-6

V7X_LANES = 128
V7X_SUBLANES = 8
V7X_VMEM_LIMIT_BYTES = 56 * 1024 * 1024

SEQ_TILE = 512
CONV_HALO = 32
CONV_CHUNK = 32
ROW_SLABS = D_MODEL // V7X_LANES
EXPERT_TILE = 256
ATT_UNROLL = 4
DMA_THREADS = 2
LOG2E = math.log2(math.e)
N_PAIRS = EXPERTS_PER_GROUP * (EXPERTS_PER_GROUP - 1) // 2
N_CLASSES = N_EXPERT_GROUPS * N_PAIRS
CLASS_ROWS = 32
NEG_BIG = -1e30

_PAIRS = [(a, b) for a in range(EXPERTS_PER_GROUP) for b in range(a + 1, EXPERTS_PER_GROUP)]


def _cparams(*sem):
    return pltpu.CompilerParams(dimension_semantics=sem, vmem_limit_bytes=V7X_VMEM_LIMIT_BYTES)


def _rms(x, g):
    return x * lax.rsqrt(jnp.mean(x * x, axis=-1, keepdims=True) + EPS) * g


def _load_row8(ref, n):
    return jnp.concatenate([ref[pl.ds(s, n, stride=ROW_SLABS), :] for s in range(ROW_SLABS)], axis=1)


def _store_row8(ref, val, n):
    for s in range(ROW_SLABS):
        ref[pl.ds(s, n, stride=ROW_SLABS), :] = val[:, s * V7X_LANES:(s + 1) * V7X_LANES]


def _bf16_bits(x):
    return lax.bitcast_convert_type(x.astype(jnp.bfloat16).astype(jnp.float32), jnp.uint32)


def _pack_pair(lo, hi):
    return (_bf16_bits(lo) >> 16) | _bf16_bits(hi)


def _unpack_lo(w):
    return lax.bitcast_convert_type(w << 16, jnp.float32)


def _unpack_hi(w):
    return lax.bitcast_convert_type(w & jnp.uint32(0xFFFF0000), jnp.float32)


def _a_layer_kernel(has_y, *refs):
    if has_y:
        x_ref, ys_ref, pos_ref, pos_next_ref, *refs = refs
    else:
        x_ref, *refs = refs
    (g_ref, win_ref, bin_ref, wdw_ref, bdw_ref, lng_ref, lnb_ref, wout_ref, bout_ref, fg_ref, wr_ref,
     o_ref, hp_ref, cls_ref, rank_ref, cnt_ref, ext_ref, sh_ref, conv_ref, carry_ref, *gather_scratch) = refs
    ts, d = x_ref.shape
    step = pl.program_id(0) * pl.num_programs(1) + pl.program_id(1)

    @pl.when(pl.program_id(1) == 0)
    def _():
        ext_ref[0:CONV_HALO, :] = jnp.zeros((CONV_HALO, d), jnp.float32)

    x = x_ref[...]
    if has_y:
        n_steps = pl.num_programs(0) * pl.num_programs(1)
        x = x + _gathered_rows(step, n_steps, ys_ref, pos_ref, pos_next_ref, *gather_scratch)
    h = _rms(x, g_ref[...])
    u = jnp.dot(h.astype(jnp.bfloat16), win_ref[...], preferred_element_type=jnp.float32) + bin_ref[...]
    ext_ref[CONV_HALO:CONV_HALO + ts, :] = u[:, :d] * jax.nn.sigmoid(u[:, d:])

    base = CONV_HALO - (CONV_KERNEL - 1)
    nsh = ts + CONV_HALO - V7X_SUBLANES
    for j in range(1, V7X_SUBLANES):
        sh_ref[j - 1] = ext_ref[pl.ds(j, nsh), :]

    def chunk(c, carry):
        r = pl.multiple_of(c * CONV_CHUNK, CONV_CHUNK)
        acc = jnp.broadcast_to(bdw_ref[...], (CONV_CHUNK, d))
        for k in range(CONV_KERNEL):
            q, j = divmod(base + k, V7X_SUBLANES)
            rows = pl.ds(r + q * V7X_SUBLANES, CONV_CHUNK)
            tap = ext_ref[rows, :] if j == 0 else sh_ref[j - 1, rows, :]
            acc = acc + wdw_ref[k:k + 1, :] * tap
        conv_ref[pl.ds(r, CONV_CHUNK), :] = acc
        return carry

    lax.fori_loop(0, ts // CONV_CHUNK, chunk, 0)
    ext_ref[0:CONV_HALO, :] = ext_ref[ts:ts + CONV_HALO, :]

    c = conv_ref[...]
    mu = jnp.mean(c, axis=-1, keepdims=True)
    cc = c - mu
    var = jnp.mean(cc * cc, axis=-1, keepdims=True)
    v = cc * lax.rsqrt(var + EPS) * lng_ref[...] + lnb_ref[...]
    v = v * jax.nn.sigmoid(v)
    out = jnp.dot(v.astype(jnp.bfloat16), wout_ref[...], preferred_element_type=jnp.float32)
    x_new = x + out + bout_ref[...]
    o_ref[...] = x_new
    _route_rows(x_new, step == 0, fg_ref, wr_ref, hp_ref, cls_ref, rank_ref, cnt_ref, carry_ref)


def _row(v):
    return v.reshape(1, -1)


def _a_layer(x2, moe, router, batch, g, w_in, b_in, w_dw, b_dw, ln_g, ln_b, w_out, b_out):
    t, d = x2.shape
    ns = t // batch // SEQ_TILE
    step_of = lambda b, s: b * ns + s
    const = lambda shape: pl.BlockSpec(shape, lambda b, s: (0, 0))
    weight = lambda shape: pl.BlockSpec(shape, lambda b, s: (0, 0), pipeline_mode=pl.Buffered(1))
    tok = pl.BlockSpec((SEQ_TILE, d), lambda b, s: (step_of(b, s), 0))
    acts, act_specs, gather_scratch = [x2], [tok], []
    if moe is not None:
        g_args, g_specs, gather_scratch = _gather_operands(moe, batch * ns, step_of)
        acts += g_args
        act_specs += g_specs
    r_args, r_specs, r_shape, r_out_specs, r_scratch = _route_operands(t, d, router, step_of)
    x_new, *routed = pl.pallas_call(
        functools.partial(_a_layer_kernel, moe is not None),
        grid=(batch, ns),
        in_specs=act_specs + [
            const((1, d)), weight((d, 2 * d)), const((1, 2 * d)), const((CONV_KERNEL, d)), const((1, d)),
            const((1, d)), const((1, d)), weight((d, d)), const((1, d)),
        ] + r_specs,
        out_specs=[tok] + r_out_specs,
        out_shape=[jax.ShapeDtypeStruct((t, d), jnp.float32)] + r_shape,
        scratch_shapes=[pltpu.VMEM((CONV_HALO + SEQ_TILE, d), jnp.float32),
                        pltpu.VMEM((V7X_SUBLANES - 1, CONV_HALO + SEQ_TILE - V7X_SUBLANES, d), jnp.float32),
                        pltpu.VMEM((SEQ_TILE, d), jnp.float32)] + r_scratch + gather_scratch,
        compiler_params=_cparams("arbitrary", "arbitrary"),
        name="a_layer",
    )(*acts, _row(g), w_in.astype(jnp.bfloat16), _row(b_in), w_dw, _row(b_dw), _row(ln_g), _row(ln_b),
      w_out.astype(jnp.bfloat16), _row(b_out), *r_args)
    return x_new, routed


def _route_rows(x, first_step, g_ref, wr_ref, hp_ref, cls_ref, rank_ref, cnt_ref, carry_ref):
    tr = x.shape[0]

    @pl.when(first_step)
    def _():
        carry_ref[...] = jnp.zeros_like(carry_ref)

    h = _rms(x, g_ref[...])
    lt = lax.dot_general(wr_ref[...], h.astype(jnp.bfloat16), (((1,), (1,)), ((), ())),
                         preferred_element_type=jnp.float32)
    lg = [lt[i:i + 1, :] for i in range(N_EXPERT_GROUPS)]
    gmax = functools.reduce(jnp.maximum, lg)
    g_idx = jnp.full_like(gmax, N_EXPERT_GROUPS - 1, dtype=jnp.int32)
    for i in reversed(range(N_EXPERT_GROUPS - 1)):
        g_idx = jnp.where(lg[i] == gmax, i, g_idx)
    p_top = 1.0 / functools.reduce(jnp.add, [jnp.exp(v - gmax) for v in lg])
    sel = []
    for e in range(EXPERTS_PER_GROUP):
        v = lt[N_EXPERT_GROUPS + e:N_EXPERT_GROUPS + e + 1, :]
        for g in range(1, N_EXPERT_GROUPS):
            r = N_EXPERT_GROUPS + g * EXPERTS_PER_GROUP + e
            v = jnp.where(g_idx == g, lt[r:r + 1, :], v)
        sel.append(v)
    v1 = functools.reduce(jnp.maximum, sel)
    i1 = jnp.full_like(g_idx, EXPERTS_PER_GROUP - 1)
    for e in reversed(range(EXPERTS_PER_GROUP - 1)):
        i1 = jnp.where(sel[e] == v1, e, i1)
    rest = [jnp.where(i1 == e, -jnp.inf, sel[e]) for e in range(EXPERTS_PER_GROUP)]
    v2 = functools.reduce(jnp.maximum, rest)
    i2 = jnp.full_like(g_idx, EXPERTS_PER_GROUP - 1)
    for e in reversed(range(EXPERTS_PER_GROUP - 1)):
        i2 = jnp.where((rest[e] == v2) & (i1 != e), e, i2)
    i2 = jnp.where((i2 == i1), jnp.where(i1 == EXPERTS_PER_GROUP - 1, EXPERTS_PER_GROUP - 2, i2), i2)
    t2 = jnp.exp(v2 - v1)
    w1 = p_top / (1.0 + t2)
    w2 = p_top * t2 / (1.0 + t2)
    first_lo = i1 < i2
    e_lo = jnp.where(first_lo, i1, i2)
    e_hi = jnp.where(first_lo, i2, i1)
    w_lo = jnp.where(first_lo, w1, w2)
    w_hi = jnp.where(first_lo, w2, w1)
    pair_base = jnp.where(e_lo == 0, 0, jnp.where(e_lo == 1, EXPERTS_PER_GROUP - 1, 2 * EXPERTS_PER_GROUP - 3))
    cls = g_idx * N_PAIRS + pair_base + (e_hi - e_lo - 1)

    onehot = (lax.broadcasted_iota(jnp.int32, (CLASS_ROWS, tr), 0) == cls)
    before = (lax.broadcasted_iota(jnp.int32, (tr, tr), 0) < lax.broadcasted_iota(jnp.int32, (tr, tr), 1))
    cum = jnp.dot(onehot.astype(jnp.bfloat16), before.astype(jnp.bfloat16), preferred_element_type=jnp.float32)
    oh = onehot.astype(jnp.float32)
    rank = jnp.sum(oh * (cum + carry_ref[...]), axis=0, keepdims=True)
    carry_ref[...] = carry_ref[...] + jnp.sum(oh, axis=1, keepdims=True)
    cls_ref[0] = cls
    rank_ref[0] = rank.astype(jnp.int32)
    cnt_ref[...] = jnp.broadcast_to(carry_ref[...], cnt_ref.shape)

    half = D_MODEL // 2
    word = _pack_pair(h[:, :half], h[:, half:])
    n_word = half // V7X_LANES
    for s in range(n_word):
        hp_ref[pl.ds(s, tr, stride=ROW_SLABS), :] = word[:, s * V7X_LANES:(s + 1) * V7X_LANES]
    wcols = jnp.concatenate([w_lo, w_hi, jnp.zeros((V7X_LANES - 2, tr), jnp.float32)], axis=0).T
    hp_ref[pl.ds(n_word, tr, stride=ROW_SLABS), :] = lax.bitcast_convert_type(wcols, jnp.uint32)
    for s in range(n_word + 1, ROW_SLABS):
        hp_ref[pl.ds(s, tr, stride=ROW_SLABS), :] = jnp.zeros((tr, V7X_LANES), jnp.uint32)


def _route_operands(t, d, router, step_of):
    g, r_grp, r_exp = router
    nt = t // SEQ_TILE
    wr = jnp.concatenate([r_grp, jnp.transpose(r_exp, (1, 0, 2)).reshape(d, N_EXPERTS)], axis=1)
    wr = jnp.pad(wr, ((0, 0), (0, CLASS_ROWS - wr.shape[1]))).T.astype(jnp.bfloat16)
    const = lambda shape: pl.BlockSpec(shape, lambda *g_: (0, 0))
    vec = lambda dt: jax.ShapeDtypeStruct((nt, 1, SEQ_TILE), dt)
    vec_spec = pl.BlockSpec((1, 1, SEQ_TILE), lambda *g_: (step_of(*g_), 0, 0))
    return ([_row(g), wr], [const((1, d)), const((CLASS_ROWS, d))],
            [jax.ShapeDtypeStruct((t * ROW_SLABS, V7X_LANES), jnp.uint32), vec(jnp.int32), vec(jnp.int32),
             jax.ShapeDtypeStruct((CLASS_ROWS, V7X_LANES), jnp.float32)],
            [pl.BlockSpec((SEQ_TILE * ROW_SLABS, V7X_LANES), lambda *g_: (step_of(*g_), 0)), vec_spec, vec_spec,
             const((CLASS_ROWS, V7X_LANES))],
            [pltpu.VMEM((CLASS_ROWS, 1), jnp.float32)])


def _row_slice(ref, row):
    return ref.at[pl.ds(pl.multiple_of(row * ROW_SLABS, ROW_SLABS), ROW_SLABS)]


def _scatter_kernel(pos_ref, pad_ref, src_ref, dst_ref, zero_ref, sem, zero_sem):
    n = pos_ref.shape[2]

    def zero_copy(row):
        return pltpu.make_async_copy(zero_ref, _row_slice(dst_ref, row), zero_sem)

    def for_each_pad_row(fn):
        for c in range(N_CLASSES):
            first = pad_ref[0, 0, c]
            lax.fori_loop(0, pad_ref[0, 1, c], lambda k, carry: fn(first + k) or carry, 0)

    @pl.when(pl.program_id(0) == 0)
    def _():
        zero_ref[...] = jnp.zeros_like(zero_ref)
        for_each_pad_row(lambda row: zero_copy(row).start())

    def issue(i, carry):
        for u in range(DMA_THREADS):
            k = i * DMA_THREADS + u
            pltpu.make_async_copy(_row_slice(src_ref, k), _row_slice(dst_ref, pos_ref[0, 0, k]),
                                  sem).start(priority=u)
        return carry

    lax.fori_loop(0, n // DMA_THREADS, issue, 0)
    pltpu.make_async_copy(src_ref, dst_ref.at[pl.ds(0, n * ROW_SLABS)], sem).wait()

    @pl.when(pl.program_id(0) == 0)
    def _():
        for_each_pad_row(lambda row: zero_copy(row).wait())


def _row_scatter(src, pos3, pad, n_dst_rows):
    nt, _, n = pos3.shape
    return pl.pallas_call(
        _scatter_kernel,
        grid=(nt,),
        in_specs=[pl.BlockSpec((1, 1, n), lambda i: (i, 0, 0), memory_space=pltpu.SMEM),
                  pl.BlockSpec(pad.shape, lambda i: (0, 0, 0), memory_space=pltpu.SMEM),
                  pl.BlockSpec((n * ROW_SLABS, V7X_LANES), lambda i: (i, 0))],
        out_specs=pl.BlockSpec(memory_space=pl.ANY),
        out_shape=jax.ShapeDtypeStruct((n_dst_rows * ROW_SLABS, V7X_LANES), src.dtype),
        scratch_shapes=[pltpu.VMEM((ROW_SLABS, V7X_LANES), src.dtype), pltpu.SemaphoreType.DMA,
                        pltpu.SemaphoreType.DMA],
        compiler_params=_cparams("arbitrary"),
        name="row_scatter",
    )(pos3, pad, src)


def _gather_copy(ys_ref, pos_ref, buf_ref, sem, slot, k):
    return pltpu.make_async_copy(_row_slice(ys_ref, pos_ref[0, 0, k]), _row_slice(buf_ref.at[slot], k), sem.at[slot])


def _gathered_rows(step, n_steps, ys_ref, pos_ref, pos_next_ref, buf_ref, sem):
    slot = step % 2
    n_tok = pos_ref.shape[2]

    @pl.when(step == 0)
    def _():
        def issue(k, carry):
            _gather_copy(ys_ref, pos_ref, buf_ref, sem, 0, k).start()
            return carry

        lax.fori_loop(0, n_tok, issue, 0)

    def wait(s):
        pltpu.make_async_copy(ys_ref.at[pl.ds(0, n_tok * ROW_SLABS)], buf_ref.at[s], sem.at[s]).wait()

    wait(slot)
    y = _load_row8(buf_ref.at[slot], n_tok)
    for k in range(n_tok):
        _gather_copy(ys_ref, pos_next_ref, buf_ref, sem, 1 - slot, k).start(priority=k % DMA_THREADS)

    @pl.when(step == n_steps - 1)
    def _():
        wait(1 - slot)

    return y


def _gather_operands(moe, n_steps, step_of):
    ys, pos3 = moe
    n = pos3.shape[2]
    smem = lambda m: pl.BlockSpec((1, 1, n), m, memory_space=pltpu.SMEM)
    cur = lambda *g: (step_of(*g), 0, 0)
    nxt = lambda *g: (jnp.minimum(step_of(*g) + 1, n_steps - 1), 0, 0)
    return ([ys, pos3, pos3], [pl.BlockSpec(memory_space=pl.ANY), smem(cur), smem(nxt)],
            [pltpu.VMEM((2, n * ROW_SLABS, V7X_LANES), jnp.float32), pltpu.SemaphoreType.DMA((2,))])


def _expert_kernel(lo_ref, hi_ref, nt_ref, hs_ref, wg_lo, wu_lo, wd_lo, wg_hi, wu_hi, wd_hi, y_ref):
    tm = EXPERT_TILE
    j = pl.program_id(0)

    @pl.when(j < nt_ref[0])
    def _():
        n_word = D_MODEL // 2 // V7X_LANES
        words = [hs_ref[pl.ds(s, tm, stride=ROW_SLABS), :] for s in range(n_word)]
        h = jnp.concatenate([_unpack_lo(w) for w in words] + [_unpack_hi(w) for w in words], axis=1)
        h = h.astype(jnp.bfloat16)
        gates = lax.bitcast_convert_type(hs_ref[pl.ds(n_word, tm, stride=ROW_SLABS), :], jnp.float32)
        y = None
        for col, (wg, wu, wd) in enumerate(((wg_lo, wu_lo, wd_lo), (wg_hi, wu_hi, wd_hi))):
            a = jnp.dot(h, wg[0], preferred_element_type=jnp.float32)
            u = jnp.dot(h, wu[0], preferred_element_type=jnp.float32)
            hid = a * jax.nn.sigmoid(a) * u * gates[:, col:col + 1]
            part = jnp.dot(hid.astype(jnp.bfloat16), wd[0], preferred_element_type=jnp.float32)
            y = part if y is None else y + part
        _store_row8(y_ref, y, tm)

    @pl.when(j >= nt_ref[0])
    def _():
        y_ref[...] = jnp.zeros_like(y_ref)


def _moe_experts(hs, tile_lo, tile_hi, n_tiles, w_gate, w_up, w_down):
    n_rows = hs.shape[0] // ROW_SLABS
    nt_max = n_rows // EXPERT_TILE
    d, f = w_gate.shape[1], w_gate.shape[2]
    act = lambda j, lo, hi, nt: (jnp.maximum(jnp.minimum(j, nt[0] - 1), 0), 0)
    w_lo = lambda j, lo, hi, nt: (lo[j], 0, 0)
    w_hi = lambda j, lo, hi, nt: (hi[j], 0, 0)
    up_spec = lambda m: pl.BlockSpec((1, d, f), m)
    dn_spec = lambda m: pl.BlockSpec((1, f, d), m)
    blk = (EXPERT_TILE * ROW_SLABS, V7X_LANES)
    wg, wu, wd = (w.astype(jnp.bfloat16) for w in (w_gate, w_up, w_down))
    return pl.pallas_call(
        _expert_kernel,
        grid_spec=pltpu.PrefetchScalarGridSpec(
            num_scalar_prefetch=3,
            grid=(nt_max,),
            in_specs=[pl.BlockSpec(blk, act), up_spec(w_lo), up_spec(w_lo), dn_spec(w_lo),
                      up_spec(w_hi), up_spec(w_hi), dn_spec(w_hi)],
            out_specs=pl.BlockSpec(blk, lambda j, lo, hi, nt: (j, 0)),
        ),
        out_shape=jax.ShapeDtypeStruct((n_rows * ROW_SLABS, V7X_LANES), jnp.float32),
        compiler_params=_cparams("arbitrary"),
        name="moe_experts",
    )(tile_lo, tile_hi, n_tiles, hs, wg, wu, wd, wg, wu, wd)


def _moe(routed, w_gate, w_up, w_down):
    hp, cls, rank, cnt = routed
    t = hp.shape[0] // ROW_SLABS
    counts = cnt[:N_CLASSES, 0].astype(jnp.int32)
    tiles_per = (counts + EXPERT_TILE - 1) // EXPERT_TILE
    tile_end = jnp.cumsum(tiles_per)
    base = (tile_end - tiles_per) * EXPERT_TILE
    cls = cls.reshape(t)
    pos = jnp.sum(jnp.where(cls[:, None] == jnp.arange(N_CLASSES)[None, :], base[None, :], 0), axis=1)
    pos = (pos + rank.reshape(t)).astype(jnp.int32)
    nt_max = t // EXPERT_TILE + N_CLASSES
    tile_cls = jnp.sum(jnp.arange(nt_max)[:, None] >= tile_end[None, :], axis=1)
    tile_cls = jnp.minimum(tile_cls, N_CLASSES - 1)
    lo_tab = jnp.array([p[0] for p in _PAIRS], jnp.int32)
    hi_tab = jnp.array([p[1] for p in _PAIRS], jnp.int32)
    tile_g = tile_cls // N_PAIRS
    tile_lo = (tile_g * EXPERTS_PER_GROUP + lo_tab[tile_cls % N_PAIRS]).astype(jnp.int32)
    tile_hi = (tile_g * EXPERTS_PER_GROUP + hi_tab[tile_cls % N_PAIRS]).astype(jnp.int32)
    n_tiles = tile_end[-1:].astype(jnp.int32)
    n_sorted = nt_max * EXPERT_TILE
    pos3 = pos.reshape(t // SEQ_TILE, 1, SEQ_TILE)
    pad = jnp.stack([base + counts, tiles_per * EXPERT_TILE - counts]).astype(jnp.int32)
    pad = jnp.pad(pad, ((0, 0), (0, CLASS_ROWS - N_CLASSES)))[None]
    hs = _row_scatter(hp, pos3, pad, n_sorted)
    ys = _moe_experts(hs, tile_lo, tile_hi, n_tiles, w_gate, w_up, w_down)
    return ys, pos3


def _rope_tables(seq):
    pos = jnp.arange(seq, dtype=jnp.float32)
    inv_freq = ROPE_THETA ** (-jnp.arange(0, HEAD_DIM, 2, dtype=jnp.float32) / HEAD_DIM)
    ang = pos[:, None] * inv_freq[None, :]
    reps = V7X_LANES // (HEAD_DIM // 2)
    cos = jnp.tile(jnp.cos(ang), (1, reps))
    sign = jnp.tile(jnp.concatenate([-jnp.ones(HEAD_DIM // 2), jnp.ones(HEAD_DIM // 2)]), V7X_LANES // HEAD_DIM)
    sin = jnp.tile(jnp.sin(ang), (1, reps)) * sign[None, :]
    return cos, sin


def _rope_slab(v, cos, sin, lane):
    half = HEAD_DIM // 2
    partner = jnp.where((lane % HEAD_DIM) < half, pltpu.roll(v, V7X_LANES - half, axis=1), pltpu.roll(v, half, axis=1))
    return v * cos + partner * sin


def _proj_kernel(has_y, specs, *refs):
    n = len(specs)
    if has_y:
        x_ref, ys_ref, pos_ref, pos_next_ref, *refs = refs
    else:
        x_ref, *refs = refs
    cos_ref, sin_ref, *refs = refs
    g_refs, w_refs, refs = refs[:n], refs[n:2 * n], refs[2 * n:]
    if has_y:
        xo_ref, *refs = refs
    o_refs, gather_scratch = refs[:n], refs[n:]
    ts = x_ref.shape[0]
    x = x_ref[...]
    if has_y:
        x = x + _gathered_rows(pl.program_id(0), pl.num_programs(0), ys_ref, pos_ref, pos_next_ref, *gather_scratch)
        xo_ref[...] = x
    inv = lax.rsqrt(jnp.mean(x * x, axis=-1, keepdims=True) + EPS)
    xn = x * inv
    cos, sin = cos_ref[...], sin_ref[...]
    lane = lax.broadcasted_iota(jnp.int32, (ts, V7X_LANES), 1)
    for i, (use_rope, scale) in enumerate(specs):
        h = (xn * g_refs[i][...]).astype(jnp.bfloat16)
        for grp in range(N_DGROUPS):
            acc = jnp.dot(h, w_refs[i][:, grp * D_MODEL:(grp + 1) * D_MODEL], preferred_element_type=jnp.float32)
            slabs = []
            for s in range(ROW_SLABS):
                v = acc[:, s * V7X_LANES:(s + 1) * V7X_LANES]
                if use_rope:
                    v = _rope_slab(v, cos, sin, lane)
                if scale != 1.0:
                    v = v * scale
                slabs.append(v)
            for q in range(ROW_SLABS // 2):
                o_refs[i][grp, :, q * V7X_LANES:(q + 1) * V7X_LANES] = _pack_pair(slabs[2 * q], slabs[2 * q + 1])


def _qkv_proj(x2, moe, seq, projections):
    t, d = x2.shape
    ns = seq // SEQ_TILE
    n = len(projections)
    cos, sin = _rope_tables(seq)
    tok = pl.BlockSpec((SEQ_TILE, d), lambda i: (i, 0))
    rope_spec = pl.BlockSpec((SEQ_TILE, V7X_LANES), lambda i: (i % ns, 0))
    acts, act_specs, gather_scratch = [x2], [tok], []
    if moe is not None:
        g_args, g_specs, gather_scratch = _gather_operands(moe, t // SEQ_TILE, lambda i: i)
        acts += g_args
        act_specs += g_specs
    packed = jax.ShapeDtypeStruct((N_DGROUPS, t, d // 2), jnp.uint32)
    packed_spec = pl.BlockSpec((N_DGROUPS, SEQ_TILE, d // 2), lambda i: (0, i, 0))
    out_shape, out_specs = [packed] * n, [packed_spec] * n
    if moe is not None:
        out_shape = [jax.ShapeDtypeStruct((t, d), jnp.float32)] + out_shape
        out_specs = [tok] + out_specs
    specs = tuple((p[3], p[4]) for p in projections)
    w_spec = lambda col: pl.BlockSpec((d, N_DGROUPS * d), lambda i: (0, col), pipeline_mode=pl.Buffered(1))
    bf16_of = {}
    for p in projections:
        bf16_of.setdefault(id(p[1]), p[1].astype(jnp.bfloat16))
    return pl.pallas_call(
        functools.partial(_proj_kernel, moe is not None, specs),
        grid=(t // SEQ_TILE,),
        in_specs=act_specs + [rope_spec, rope_spec]
                 + [pl.BlockSpec((1, d), lambda i: (0, 0))] * n
                 + [w_spec(p[2]) for p in projections],
        out_specs=out_specs,
        out_shape=out_shape,
        scratch_shapes=gather_scratch,
        compiler_params=_cparams("arbitrary"),
        name="qkv_proj",
    )(*acts, cos, sin, *[_row(p[0]) for p in projections], *[bf16_of[id(p[1])] for p in projections])


def _attn_unit(pair, qw, kw, vw, bias, lane):
    unpack = _unpack_lo if pair == 0 else _unpack_hi
    q = unpack(qw).astype(jnp.bfloat16)
    k = unpack(kw).astype(jnp.bfloat16)
    v = unpack(vw).astype(jnp.bfloat16)
    first = lane < HEAD_DIM
    zero = jnp.zeros_like(q)
    qs = jnp.concatenate([jnp.where(first, q, zero), jnp.where(first, zero, q)], axis=0)
    s = lax.dot_general(qs, k, (((1,), (1,)), ((), ())), preferred_element_type=jnp.float32) + bias
    m = jnp.max(s, axis=1, keepdims=True)
    p = jnp.exp2(s - m)
    l = jnp.sum(p, axis=1, keepdims=True)
    pv = jnp.dot(p.astype(jnp.bfloat16), v, preferred_element_type=jnp.float32)
    nq = ATT_BLOCK
    o = jnp.where(first, pv[:nq], pv[nq:])
    shape = (nq, V7X_LANES)
    m2 = jnp.where(first, jnp.broadcast_to(m[:nq], shape), jnp.broadcast_to(m[nq:], shape))
    l2 = jnp.where(first, jnp.broadcast_to(l[:nq], shape), jnp.broadcast_to(l[nq:], shape))
    return o, m2, l2


def _attn_kernel(q_ref, k_ref, v_ref, o_ref, acc_ref, m_ref, l_ref):
    seq = q_ref.shape[1]
    nq = ATT_BLOCK
    lane = lax.broadcasted_iota(jnp.int32, (nq, V7X_LANES), 1)
    qi = lax.broadcasted_iota(jnp.int32, (2 * nq, 2 * nq), 0) % nq
    kj = lax.broadcasted_iota(jnp.int32, (2 * nq, 2 * nq), 1)
    band_prev = (kj < nq) & (kj >= qi)
    band_cur = (kj >= nq) & (kj - nq <= qi)
    bias_full = jnp.where(band_prev | band_cur, 0.0, NEG_BIG)
    bias_cur = bias_full[:, nq:]
    prev_cols = jnp.where(kj < nq, NEG_BIG, 0.0)

    for grp, (_, dil) in enumerate(DILATED_GROUPS):
        nblk = seq // dil // nq
        assert nblk % ATT_UNROLL == 0 or ATT_UNROLL % nblk == 0

        def rows(start):
            return pl.ds(pl.multiple_of(start, nq), nq) if dil == 1 else pl.ds(start, nq, stride=dil)

        def unit(it, u):
            idx = it * ATT_UNROLL + u
            n = u % nblk if ATT_UNROLL % nblk == 0 else idx % nblk
            r = idx // nblk
            start = n * nq * dil + r
            qw = q_ref[grp, rows(start), :]
            kw, vw = k_ref[grp, rows(start), :], v_ref[grp, rows(start), :]
            if isinstance(n, int) and n == 0:
                bias = bias_cur
            else:
                prev = jnp.maximum(n - 1, 0) * nq * dil + r
                kw = jnp.concatenate([k_ref[grp, rows(prev), :], kw], axis=0)
                vw = jnp.concatenate([v_ref[grp, rows(prev), :], vw], axis=0)
                bias = bias_full
                if not isinstance(n, int) and u == 0:
                    bias = bias + jnp.where(n > 0, 0.0, 1.0) * prev_cols
            for pair in range(2):
                o, m2, l2 = _attn_unit(pair, qw, kw, vw, bias, lane)
                if grp == 0:
                    acc_ref[pair, rows(start), :] = o
                    m_ref[pair, rows(start), :] = m2
                    l_ref[pair, rows(start), :] = l2
                else:
                    m_old = m_ref[pair, rows(start), :]
                    m_new = jnp.maximum(m_old, m2)
                    a_old, a_new = jnp.exp2(m_old - m_new), jnp.exp2(m2 - m_new)
                    acc_ref[pair, rows(start), :] = acc_ref[pair, rows(start), :] * a_old + o * a_new
                    l_ref[pair, rows(start), :] = l_ref[pair, rows(start), :] * a_old + l2 * a_new
                    m_ref[pair, rows(start), :] = m_new

        def units(it, carry):
            for u in range(ATT_UNROLL):
                unit(it, u)
            return carry

        lax.fori_loop(0, seq // nq // ATT_UNROLL, units, 0)

    for pair in range(2):
        o_ref[:, pair * V7X_LANES:(pair + 1) * V7X_LANES] = (acc_ref[pair] / l_ref[pair]).astype(o_ref.dtype)


def _dilated_attention(qp, kp, vp, batch):
    t = qp.shape[1]
    seq = t // batch
    nquad = qp.shape[2] // V7X_LANES
    spec = pl.BlockSpec((N_DGROUPS, seq, V7X_LANES), lambda b, c: (0, b, c))
    return pl.pallas_call(
        _attn_kernel,
        grid=(batch, nquad),
        in_specs=[spec, spec, spec],
        out_specs=pl.BlockSpec((seq, 2 * V7X_LANES), lambda b, c: (b, c)),
        out_shape=jax.ShapeDtypeStruct((t, D_MODEL), jnp.bfloat16),
        scratch_shapes=[pltpu.VMEM((2, seq, V7X_LANES), jnp.float32)] * 3,
        compiler_params=_cparams("arbitrary", "arbitrary"),
        name="dilated_attention",
    )(qp, kp, vp)


def _attn_out_kernel(x_ref, c_ref, w_ref, fg_ref, wr_ref, o_ref, hp_ref, cls_ref, rank_ref, cnt_ref, carry_ref):
    x_new = x_ref[...] + jnp.dot(c_ref[...], w_ref[...], preferred_element_type=jnp.float32)
    o_ref[...] = x_new
    _route_rows(x_new, pl.program_id(0) == 0, fg_ref, wr_ref, hp_ref, cls_ref, rank_ref, cnt_ref, carry_ref)


def _attn_out(x2, comb, w_o, router):
    t, d = x2.shape
    tok = pl.BlockSpec((SEQ_TILE, d), lambda i: (i, 0))
    r_args, r_specs, r_shape, r_out_specs, r_scratch = _route_operands(t, d, router, lambda i: i)
    x_new, *routed = pl.pallas_call(
        _attn_out_kernel,
        grid=(t // SEQ_TILE,),
        in_specs=[tok, tok, pl.BlockSpec((d, d), lambda i: (0, 0))] + r_specs,
        out_specs=[tok] + r_out_specs,
        out_shape=[jax.ShapeDtypeStruct((t, d), jnp.float32)] + r_shape,
        scratch_shapes=r_scratch,
        compiler_params=_cparams("arbitrary"),
        name="attn_out",
    )(x2, comb, w_o.astype(jnp.bfloat16), *r_args)
    return x_new, routed


def _final_kernel(x_ref, ys_ref, pos_ref, pos_next_ref, g_ref, o_ref, *gather_scratch):
    y = _gathered_rows(pl.program_id(0), pl.num_programs(0), ys_ref, pos_ref, pos_next_ref, *gather_scratch)
    o_ref[...] = _rms(x_ref[...] + y, g_ref[...])


def _final_norm(x2, moe, g):
    t, d = x2.shape
    tok = pl.BlockSpec((SEQ_TILE, d), lambda i: (i, 0))
    g_args, g_specs, gather_scratch = _gather_operands(moe, t // SEQ_TILE, lambda i: i)
    return pl.pallas_call(
        _final_kernel,
        grid=(t // SEQ_TILE,),
        in_specs=[tok] + g_specs + [pl.BlockSpec((1, d), lambda i: (0, 0))],
        out_specs=tok,
        out_shape=jax.ShapeDtypeStruct((t, d), jnp.float32),
        scratch_shapes=gather_scratch,
        compiler_params=_cparams("arbitrary"),
        name="final_norm",
    )(x2, *g_args, _row(g))


def kernel(x, a_norm, a_w_in, a_b_in, a_w_dw, a_b_dw, a_ln_g, a_ln_b, a_w_out, a_b_out, kv_norm, w_kv, b_norm,
           b_w_q, b_w_o, ffn_norm, router_group, router_expert, w_gate, w_up, w_down, final_norm):
    batch, seq, d = x.shape
    n_a = a_norm.shape[0]
    depth = ffn_norm.shape[0]
    q_scale = HEAD_DIM ** -0.5 * LOG2E
    x2 = x.reshape(batch * seq, d)
    moe = None
    kp = vp = None
    for l in range(depth):
        router = (ffn_norm[l], router_group[l], router_expert[l])
        if l < n_a:
            x2, routed = _a_layer(x2, moe, router, batch, a_norm[l], a_w_in[l], a_b_in[l], a_w_dw[l], a_b_dw[l],
                                  a_ln_g[l], a_ln_b[l], a_w_out[l], a_b_out[l])
        else:
            j = l - n_a
            projections = [(b_norm[j], b_w_q[j], 0, True, q_scale)]
            if kp is None:
                projections += [(kv_norm, w_kv, 0, True, 1.0), (kv_norm, w_kv, 1, False, 1.0)]
            outs = _qkv_proj(x2, moe, seq, projections)
            if moe is not None:
                x2, *outs = outs
            if kp is None:
                qp, kp, vp = outs
            else:
                qp, = outs
            comb = _dilated_attention(qp, kp, vp, batch)
            x2, routed = _attn_out(x2, comb, b_w_o[j], router)
        moe = _moe(routed, w_gate[l], w_up[l], w_down[l])
    return _final_norm(x2, moe, final_norm).reshape(batch, seq, d)
```

```python
import functools
import math

import jax
import jax.numpy as jnp
import numpy as np
from jax import lax
from jax.experimental import pallas as pl
from jax.experimental.pallas import tpu as pltpu

D_MODEL = 1024
CONV_KERNEL = 31
HEAD_DIM = 64
N_HEADS = 16
DILATED_GROUPS = ((128, 1), (512, 4), (2048, 16))
N_DGROUPS = len(DILATED_GROUPS)
ATT_BLOCK = 128
ROPE_THETA = 10000.0
N_EXPERT_GROUPS = 4
EXPERTS_PER_GROUP = 4
N_EXPERTS = N_EXPERT_GROUPS * EXPERTS_PER_GROUP
D_EXPERT = D_MODEL // 2
EPS = 1e-6

V7X_LANES = 128
V7X_SUBLANES = 8
V7X_VMEM_LIMIT_BYTES = 56 * 1024 * 1024

SEQ_TILE = 512
CONV_HALO = 32
CONV_CHUNK = 32
ROW_SLABS = D_MODEL // V7X_LANES
EXPERT_TILE = 256
ATT_UNROLL = 8
DMA_THREADS = 2
LOG2E = math.log2(math.e)
N_PAIRS = EXPERTS_PER_GROUP * (EXPERTS_PER_GROUP - 1) // 2
N_CLASSES = N_EXPERT_GROUPS * N_PAIRS
CLASS_ROWS = 32
NEG_BIG = -1e30

_PAIRS = [(a, b) for a in range(EXPERTS_PER_GROUP) for b in range(a + 1, EXPERTS_PER_GROUP)]


def _cparams(*sem):
    return pltpu.CompilerParams(dimension_semantics=sem, vmem_limit_bytes=V7X_VMEM_LIMIT_BYTES)


def _rms(x, g):
    return x * lax.rsqrt(jnp.mean(x * x, axis=-1, keepdims=True) + EPS) * g


def _load_row8(ref, n):
    return jnp.concatenate([ref[pl.ds(s, n, stride=ROW_SLABS), :] for s in range(ROW_SLABS)], axis=1)


def _store_row8(ref, val, n):
    for s in range(ROW_SLABS):
        ref[pl.ds(s, n, stride=ROW_SLABS), :] = val[:, s * V7X_LANES:(s + 1) * V7X_LANES]


def _bf16_bits(x):
    return lax.bitcast_convert_type(x.astype(jnp.bfloat16).astype(jnp.float32), jnp.uint32)


def _pack_pair(lo, hi):
    return (_bf16_bits(lo) >> 16) | _bf16_bits(hi)


def _unpack_lo(w):
    return lax.bitcast_convert_type(w << 16, jnp.float32)


def _unpack_hi(w):
    return lax.bitcast_convert_type(w & jnp.uint32(0xFFFF0000), jnp.float32)


def _a_layer_kernel(has_y, *refs):
    if has_y:
        x_ref, ys_ref, pos_ref, pos_next_ref, *refs = refs
    else:
        x_ref, *refs = refs
    (g_ref, win_ref, bin_ref, wdw_ref, bdw_ref, lng_ref, lnb_ref, wout_ref, bout_ref, fg_ref, wr_ref,
     o_ref, hp_ref, cls_ref, rank_ref, cnt_ref, ext_ref, sh_ref, conv_ref, carry_ref, *gather_scratch) = refs
    ts, d = x_ref.shape
    step = pl.program_id(0) * pl.num_programs(1) + pl.program_id(1)

    @pl.when(pl.program_id(1) == 0)
    def _():
        ext_ref[0:CONV_HALO, :] = jnp.zeros((CONV_HALO, d), jnp.float32)

    x = x_ref[...]
    if has_y:
        n_steps = pl.num_programs(0) * pl.num_programs(1)
        x = x + _gathered_rows(step, n_steps, ys_ref, pos_ref, pos_next_ref, *gather_scratch)
    h = _rms(x, g_ref[...])
    u = jnp.dot(h.astype(jnp.bfloat16), win_ref[...], preferred_element_type=jnp.float32) + bin_ref[...]
    ext_ref[CONV_HALO:CONV_HALO + ts, :] = u[:, :d] * jax.nn.sigmoid(u[:, d:])

    base = CONV_HALO - (CONV_KERNEL - 1)
    nsh = ts + CONV_HALO - V7X_SUBLANES
    for j in range(1, V7X_SUBLANES):
        sh_ref[j - 1] = ext_ref[pl.ds(j, nsh), :]

    def chunk(c, carry):
        r = pl.multiple_of(c * CONV_CHUNK, CONV_CHUNK)
        acc = jnp.broadcast_to(bdw_ref[...], (CONV_CHUNK, d))
        for k in range(CONV_KERNEL):
            q, j = divmod(base + k, V7X_SUBLANES)
            rows = pl.ds(r + q * V7X_SUBLANES, CONV_CHUNK)
            tap = ext_ref[rows, :] if j == 0 else sh_ref[j - 1, rows, :]
            acc = acc + wdw_ref[k:k + 1, :] * tap
        conv_ref[pl.ds(r, CONV_CHUNK), :] = acc
        return carry

    lax.fori_loop(0, ts // CONV_CHUNK, chunk, 0)
    ext_ref[0:CONV_HALO, :] = ext_ref[ts:ts + CONV_HALO, :]

    c = conv_ref[...]
    mu = jnp.mean(c, axis=-1, keepdims=True)
    cc = c - mu
    var = jnp.mean(cc * cc, axis=-1, keepdims=True)
    v = cc * lax.rsqrt(var + EPS) * lng_ref[...] + lnb_ref[...]
    v = v * jax.nn.sigmoid(v)
    out = jnp.dot(v.astype(jnp.bfloat16), wout_ref[...], preferred_element_type=jnp.float32)
    x_new = x + out + bout_ref[...]
    o_ref[...] = x_new
    _route_rows(x_new, step == 0, fg_ref, wr_ref, hp_ref, cls_ref, rank_ref, cnt_ref, carry_ref)


def _row(v):
    return v.reshape(1, -1)


def _a_layer(x2, moe, router, batch, g, w_in, b_in, w_dw, b_dw, ln_g, ln_b, w_out, b_out):
    t, d = x2.shape
    ns = t // batch // SEQ_TILE
    step_of = lambda b, s: b * ns + s
    const = lambda shape: pl.BlockSpec(shape, lambda b, s: (0, 0))
    weight = lambda shape: pl.BlockSpec(shape, lambda b, s: (0, 0), pipeline_mode=pl.Buffered(1))
    tok = pl.BlockSpec((SEQ_TILE, d), lambda b, s: (step_of(b, s), 0))
    acts, act_specs, gather_scratch = [x2], [tok], []
    if moe is not None:
        g_args, g_specs, gather_scratch = _gather_operands(moe, batch * ns, step_of)
        acts += g_args
        act_specs += g_specs
    r_args, r_specs, r_shape, r_out_specs, r_scratch = _route_operands(t, d, router, step_of)
    x_new, *routed = pl.pallas_call(
        functools.partial(_a_layer_kernel, moe is not None),
        grid=(batch, ns),
        in_specs=act_specs + [
            const((1, d)), weight((d, 2 * d)), const((1, 2 * d)), const((CONV_KERNEL, d)), const((1, d)),
            const((1, d)), const((1, d)), weight((d, d)), const((1, d)),
        ] + r_specs,
        out_specs=[tok] + r_out_specs,
        out_shape=[jax.ShapeDtypeStruct((t, d), jnp.float32)] + r_shape,
        scratch_shapes=[pltpu.VMEM((CONV_HALO + SEQ_TILE, d), jnp.float32),
                        pltpu.VMEM((V7X_SUBLANES - 1, CONV_HALO + SEQ_TILE - V7X_SUBLANES, d), jnp.float32),
                        pltpu.VMEM((SEQ_TILE, d), jnp.float32)] + r_scratch + gather_scratch,
        compiler_params=_cparams("arbitrary", "arbitrary"),
        name="a_layer",
    )(*acts, _row(g), w_in.astype(jnp.bfloat16), _row(b_in), w_dw, _row(b_dw), _row(ln_g), _row(ln_b),
      w_out.astype(jnp.bfloat16), _row(b_out), *r_args)
    return x_new, routed


def _route_rows(x, first_step, g_ref, wr_ref, hp_ref, cls_ref, rank_ref, cnt_ref, carry_ref):
    tr = x.shape[0]

    @pl.when(first_step)
    def _():
        carry_ref[...] = jnp.zeros_like(carry_ref)

    h = _rms(x, g_ref[...])
    lt = lax.dot_general(wr_ref[...], h.astype(jnp.bfloat16), (((1,), (1,)), ((), ())),
                         preferred_element_type=jnp.float32)
    lg = [lt[i:i + 1, :] for i in range(N_EXPERT_GROUPS)]
    gmax = functools.reduce(jnp.maximum, lg)
    g_idx = jnp.full_like(gmax, N_EXPERT_GROUPS - 1, dtype=jnp.int32)
    for i in reversed(range(N_EXPERT_GROUPS - 1)):
        g_idx = jnp.where(lg[i] == gmax, i, g_idx)
    p_top = 1.0 / functools.reduce(jnp.add, [jnp.exp(v - gmax) for v in lg])
    sel = []
    for e in range(EXPERTS_PER_GROUP):
        v = lt[N_EXPERT_GROUPS + e:N_EXPERT_GROUPS + e + 1, :]
        for g in range(1, N_EXPERT_GROUPS):
            r = N_EXPERT_GROUPS + g * EXPERTS_PER_GROUP + e
            v = jnp.where(g_idx == g, lt[r:r + 1, :], v)
        sel.append(v)
    v1 = functools.reduce(jnp.maximum, sel)
    i1 = jnp.full_like(g_idx, EXPERTS_PER_GROUP - 1)
    for e in reversed(range(EXPERTS_PER_GROUP - 1)):
        i1 = jnp.where(sel[e] == v1, e, i1)
    rest = [jnp.where(i1 == e, -jnp.inf, sel[e]) for e in range(EXPERTS_PER_GROUP)]
    v2 = functools.reduce(jnp.maximum, rest)
    i2 = jnp.full_like(g_idx, EXPERTS_PER_GROUP - 1)
    for e in reversed(range(EXPERTS_PER_GROUP - 1)):
        i2 = jnp.where((rest[e] == v2) & (i1 != e), e, i2)
    i2 = jnp.where((i2 == i1), jnp.where(i1 == EXPERTS_PER_GROUP - 1, EXPERTS_PER_GROUP - 2, i2), i2)
    t2 = jnp.exp(v2 - v1)
    w1 = p_top / (1.0 + t2)
    w2 = p_top * t2 / (1.0 + t2)
    first_lo = i1 < i2
    e_lo = jnp.where(first_lo, i1, i2)
    e_hi = jnp.where(first_lo, i2, i1)
    w_lo = jnp.where(first_lo, w1, w2)
    w_hi = jnp.where(first_lo, w2, w1)
    pair_base = jnp.where(e_lo == 0, 0, jnp.where(e_lo == 1, EXPERTS_PER_GROUP - 1, 2 * EXPERTS_PER_GROUP - 3))
    cls = g_idx * N_PAIRS + pair_base + (e_hi - e_lo - 1)

    onehot = (lax.broadcasted_iota(jnp.int32, (CLASS_ROWS, tr), 0) == cls)
    before = (lax.broadcasted_iota(jnp.int32, (tr, tr), 0) < lax.broadcasted_iota(jnp.int32, (tr, tr), 1))
    cum = jnp.dot(onehot.astype(jnp.bfloat16), before.astype(jnp.bfloat16), preferred_element_type=jnp.float32)
    oh = onehot.astype(jnp.float32)
    rank = jnp.sum(oh * (cum + carry_ref[...]), axis=0, keepdims=True)
    carry_ref[...] = carry_ref[...] + jnp.sum(oh, axis=1, keepdims=True)
    cls_ref[0] = cls
    rank_ref[0] = rank.astype(jnp.int32)
    cnt_ref[...] = jnp.broadcast_to(carry_ref[...], cnt_ref.shape)

    half = D_MODEL // 2
    word = _pack_pair(h[:, :half], h[:, half:])
    n_word = half // V7X_LANES
    for s in range(n_word):
        hp_ref[pl.ds(s, tr, stride=ROW_SLABS), :] = word[:, s * V7X_LANES:(s + 1) * V7X_LANES]
    wcols = jnp.concatenate([w_lo, w_hi, jnp.zeros((V7X_LANES - 2, tr), jnp.float32)], axis=0).T
    hp_ref[pl.ds(n_word, tr, stride=ROW_SLABS), :] = lax.bitcast_convert_type(wcols, jnp.uint32)
    for s in range(n_word + 1, ROW_SLABS):
        hp_ref[pl.ds(s, tr, stride=ROW_SLABS), :] = jnp.zeros((tr, V7X_LANES), jnp.uint32)


def _route_operands(t, d, router, step_of):
    g, r_grp, r_exp = router
    nt = t // SEQ_TILE
    wr = jnp.concatenate([r_grp, jnp.transpose(r_exp, (1, 0, 2)).reshape(d, N_EXPERTS)], axis=1)
    wr = jnp.pad(wr, ((0, 0), (0, CLASS_ROWS - wr.shape[1]))).T.astype(jnp.bfloat16)
    const = lambda shape: pl.BlockSpec(shape, lambda *g_: (0, 0))
    vec = lambda dt: jax.ShapeDtypeStruct((nt, 1, SEQ_TILE), dt)
    vec_spec = pl.BlockSpec((1, 1, SEQ_TILE), lambda *g_: (step_of(*g_), 0, 0))
    return ([_row(g), wr], [const((1, d)), const((CLASS_ROWS, d))],
            [jax.ShapeDtypeStruct((t * ROW_SLABS, V7X_LANES), jnp.uint32), vec(jnp.int32), vec(jnp.int32),
             jax.ShapeDtypeStruct((CLASS_ROWS, V7X_LANES), jnp.float32)],
            [pl.BlockSpec((SEQ_TILE * ROW_SLABS, V7X_LANES), lambda *g_: (step_of(*g_), 0)), vec_spec, vec_spec,
             const((CLASS_ROWS, V7X_LANES))],
            [pltpu.VMEM((CLASS_ROWS, 1), jnp.float32)])


def _row_slice(ref, row):
    return ref.at[pl.ds(pl.multiple_of(row * ROW_SLABS, ROW_SLABS), ROW_SLABS)]


def _scatter_kernel(pos_ref, pad_ref, src_ref, dst_ref, zero_ref, sem, zero_sem):
    n = pos_ref.shape[2]
    tile_rows = EXPERT_TILE * ROW_SLABS

    def zero_copy(row):
        return pltpu.make_async_copy(zero_ref.at[pl.ds(0, ROW_SLABS)], _row_slice(dst_ref, row), zero_sem)

    def zero_tile_copy(j):
        return pltpu.make_async_copy(zero_ref, dst_ref.at[pl.ds(pl.multiple_of(j * tile_rows, tile_rows), tile_rows)],
                                     zero_sem)

    def for_each_pad_row(fn):
        for c in range(N_CLASSES):
            first = pad_ref[0, 0, c]
            lax.fori_loop(0, pad_ref[0, 1, c], lambda k, carry: fn(first + k) or carry, 0)

    def for_each_unused_tile(fn):
        lax.fori_loop(pad_ref[0, 0, N_CLASSES], dst_ref.shape[0] // tile_rows, lambda j, carry: fn(j) or carry, 0)

    @pl.when(pl.program_id(0) == 0)
    def _():
        zero_ref[...] = jnp.zeros_like(zero_ref)
        for_each_pad_row(lambda row: zero_copy(row).start())
        for_each_unused_tile(lambda j: zero_tile_copy(j).start())

    def issue(i, carry):
        for u in range(DMA_THREADS):
            k = i * DMA_THREADS + u
            pltpu.make_async_copy(_row_slice(src_ref, k), _row_slice(dst_ref, pos_ref[0, 0, k]),
                                  sem).start(priority=u)
        return carry

    lax.fori_loop(0, n // DMA_THREADS, issue, 0)
    pltpu.make_async_copy(src_ref, dst_ref.at[pl.ds(0, n * ROW_SLABS)], sem).wait()

    @pl.when(pl.program_id(0) == 0)
    def _():
        for_each_pad_row(lambda row: zero_copy(row).wait())
        for_each_unused_tile(lambda j: zero_tile_copy(j).wait())


def _row_scatter(src, pos3, pad, n_dst_rows):
    nt, _, n = pos3.shape
    return pl.pallas_call(
        _scatter_kernel,
        grid=(nt,),
        in_specs=[pl.BlockSpec((1, 1, n), lambda i: (i, 0, 0), memory_space=pltpu.SMEM),
                  pl.BlockSpec(pad.shape, lambda i: (0, 0, 0), memory_space=pltpu.SMEM),
                  pl.BlockSpec((n * ROW_SLABS, V7X_LANES), lambda i: (i, 0))],
        out_specs=pl.BlockSpec(memory_space=pl.ANY),
        out_shape=jax.ShapeDtypeStruct((n_dst_rows * ROW_SLABS, V7X_LANES), src.dtype),
        scratch_shapes=[pltpu.VMEM((EXPERT_TILE * ROW_SLABS, V7X_LANES), src.dtype), pltpu.SemaphoreType.DMA,
                        pltpu.SemaphoreType.DMA],
        compiler_params=_cparams("arbitrary"),
        name="row_scatter",
    )(pos3, pad, src)


def _gather_copy(ys_ref, pos_ref, buf_ref, sem, slot, k):
    return pltpu.make_async_copy(_row_slice(ys_ref, pos_ref[0, 0, k]), _row_slice(buf_ref.at[slot], k), sem.at[slot])


def _gathered_rows(step, n_steps, ys_ref, pos_ref, pos_next_ref, buf_ref, sem):
    slot = step % 2
    n_tok = pos_ref.shape[2]

    @pl.when(step == 0)
    def _():
        def issue(k, carry):
            _gather_copy(ys_ref, pos_ref, buf_ref, sem, 0, k).start()
            return carry

        lax.fori_loop(0, n_tok, issue, 0)

    def wait(s):
        pltpu.make_async_copy(ys_ref.at[pl.ds(0, n_tok * ROW_SLABS)], buf_ref.at[s], sem.at[s]).wait()

    wait(slot)
    y = _load_row8(buf_ref.at[slot], n_tok)
    for k in range(n_tok):
        _gather_copy(ys_ref, pos_next_ref, buf_ref, sem, 1 - slot, k).start(priority=k % DMA_THREADS)

    @pl.when(step == n_steps - 1)
    def _():
        wait(1 - slot)

    return y


def _gather_operands(moe, n_steps, step_of):
    ys, pos3 = moe
    n = pos3.shape[2]
    smem = lambda m: pl.BlockSpec((1, 1, n), m, memory_space=pltpu.SMEM)
    cur = lambda *g: (step_of(*g), 0, 0)
    nxt = lambda *g: (jnp.minimum(step_of(*g) + 1, n_steps - 1), 0, 0)
    return ([ys, pos3, pos3], [pl.BlockSpec(memory_space=pl.ANY), smem(cur), smem(nxt)],
            [pltpu.VMEM((2, n * ROW_SLABS, V7X_LANES), jnp.float32), pltpu.SemaphoreType.DMA((2,))])


def _expert_kernel(lo_ref, hi_ref, nt_ref, hs_ref, wg_lo, wu_lo, wd_lo, wg_hi, wu_hi, wd_hi, y_ref):
    tm = EXPERT_TILE
    j = pl.program_id(0)

    @pl.when(j < nt_ref[0])
    def _():
        n_word = D_MODEL // 2 // V7X_LANES
        words = [hs_ref[pl.ds(s, tm, stride=ROW_SLABS), :] for s in range(n_word)]
        h = jnp.concatenate([_unpack_lo(w) for w in words] + [_unpack_hi(w) for w in words], axis=1)
        h = h.astype(jnp.bfloat16)
        gates = lax.bitcast_convert_type(hs_ref[pl.ds(n_word, tm, stride=ROW_SLABS), :], jnp.float32)
        y = None
        for col, (wg, wu, wd) in enumerate(((wg_lo, wu_lo, wd_lo), (wg_hi, wu_hi, wd_hi))):
            a = jnp.dot(h, wg[0], preferred_element_type=jnp.float32)
            u = jnp.dot(h, wu[0], preferred_element_type=jnp.float32)
            hid = a * jax.nn.sigmoid(a) * u * gates[:, col:col + 1]
            part = jnp.dot(hid.astype(jnp.bfloat16), wd[0], preferred_element_type=jnp.float32)
            y = part if y is None else y + part
        _store_row8(y_ref, y, tm)

    @pl.when(j >= nt_ref[0])
    def _():
        y_ref[...] = jnp.zeros_like(y_ref)


def _moe_experts(hs, tile_lo, tile_hi, n_tiles, w_gate, w_up, w_down):
    n_rows = hs.shape[0] // ROW_SLABS
    nt_max = n_rows // EXPERT_TILE
    d, f = w_gate.shape[1], w_gate.shape[2]
    act = lambda j, lo, hi, nt: (jnp.maximum(jnp.minimum(j, nt[0] - 1), 0), 0)
    w_lo = lambda j, lo, hi, nt: (lo[j], 0, 0)
    w_hi = lambda j, lo, hi, nt: (hi[j], 0, 0)
    up_spec = lambda m: pl.BlockSpec((1, d, f), m)
    dn_spec = lambda m: pl.BlockSpec((1, f, d), m)
    blk = (EXPERT_TILE * ROW_SLABS, V7X_LANES)
    wg, wu, wd = w_gate, w_up, w_down
    return pl.pallas_call(
        _expert_kernel,
        grid_spec=pltpu.PrefetchScalarGridSpec(
            num_scalar_prefetch=3,
            grid=(nt_max,),
            in_specs=[pl.BlockSpec(blk, act), up_spec(w_lo), up_spec(w_lo), dn_spec(w_lo),
                      up_spec(w_hi), up_spec(w_hi), dn_spec(w_hi)],
            out_specs=pl.BlockSpec(blk, lambda j, lo, hi, nt: (j, 0)),
        ),
        out_shape=jax.ShapeDtypeStruct((n_rows * ROW_SLABS, V7X_LANES), jnp.float32),
        compiler_params=_cparams("arbitrary"),
        name="moe_experts",
    )(tile_lo, tile_hi, n_tiles, hs, wg, wu, wd, wg, wu, wd)


def _moe(routed, layer, w_gate, w_up, w_down):
    hp, cls, rank, cnt = routed
    t = hp.shape[0] // ROW_SLABS
    counts = cnt[:N_CLASSES, 0].astype(jnp.int32)
    tiles_per = (counts + EXPERT_TILE - 1) // EXPERT_TILE
    tile_end = jnp.cumsum(tiles_per)
    base = (tile_end - tiles_per) * EXPERT_TILE
    cls = cls.reshape(t)
    pos = jnp.sum(jnp.where(cls[:, None] == jnp.arange(N_CLASSES)[None, :], base[None, :], 0), axis=1)
    pos = (pos + rank.reshape(t)).astype(jnp.int32)
    nt_max = t // EXPERT_TILE + N_CLASSES
    tile_cls = jnp.sum(jnp.arange(nt_max)[:, None] >= tile_end[None, :], axis=1)
    tile_cls = jnp.minimum(tile_cls, N_CLASSES - 1)
    lo_tab = jnp.array([p[0] for p in _PAIRS], jnp.int32)
    hi_tab = jnp.array([p[1] for p in _PAIRS], jnp.int32)
    tile_g = tile_cls // N_PAIRS + layer * N_EXPERT_GROUPS
    tile_lo = (tile_g * EXPERTS_PER_GROUP + lo_tab[tile_cls % N_PAIRS]).astype(jnp.int32)
    tile_hi = (tile_g * EXPERTS_PER_GROUP + hi_tab[tile_cls % N_PAIRS]).astype(jnp.int32)
    n_tiles = tile_end[-1:].astype(jnp.int32)
    n_sorted = nt_max * EXPERT_TILE
    pos3 = pos.reshape(t // SEQ_TILE, 1, SEQ_TILE)
    pad = jnp.stack([base + counts, tiles_per * EXPERT_TILE - counts]).astype(jnp.int32)
    pad = jnp.pad(pad, ((0, 0), (0, CLASS_ROWS - N_CLASSES)))
    pad = pad.at[0, N_CLASSES].set(n_tiles[0])[None]
    hs = _row_scatter(hp, pos3, pad, n_sorted)
    ys = _moe_experts(hs, tile_lo, tile_hi, n_tiles, w_gate, w_up, w_down)
    return ys, pos3


def _rope_tables(seq):
    pos = jnp.arange(seq, dtype=jnp.float32)
    inv_freq = ROPE_THETA ** (-jnp.arange(0, HEAD_DIM, 2, dtype=jnp.float32) / HEAD_DIM)
    ang = pos[:, None] * inv_freq[None, :]
    reps = V7X_LANES // (HEAD_DIM // 2)
    cos = jnp.tile(jnp.cos(ang), (1, reps))
    sign = jnp.tile(jnp.concatenate([-jnp.ones(HEAD_DIM // 2), jnp.ones(HEAD_DIM // 2)]), V7X_LANES // HEAD_DIM)
    sin = jnp.tile(jnp.sin(ang), (1, reps)) * sign[None, :]
    return cos, sin


def _rope_slab(v, cos, sin, lane):
    half = HEAD_DIM // 2
    partner = jnp.where((lane % HEAD_DIM) < half, pltpu.roll(v, V7X_LANES - half, axis=1), pltpu.roll(v, half, axis=1))
    return v * cos + partner * sin


def _proj_kernel(has_y, specs, *refs):
    n = len(specs)
    if has_y:
        x_ref, ys_ref, pos_ref, pos_next_ref, *refs = refs
    else:
        x_ref, *refs = refs
    cos_ref, sin_ref, *refs = refs
    g_refs, w_refs, refs = refs[:n], refs[n:2 * n], refs[2 * n:]
    if has_y:
        xo_ref, *refs = refs
    o_refs, gather_scratch = refs[:n], refs[n:]
    ts = x_ref.shape[0]
    x = x_ref[...]
    if has_y:
        x = x + _gathered_rows(pl.program_id(0), pl.num_programs(0), ys_ref, pos_ref, pos_next_ref, *gather_scratch)
        xo_ref[...] = x
    inv = lax.rsqrt(jnp.mean(x * x, axis=-1, keepdims=True) + EPS)
    xn = x * inv
    cos, sin = cos_ref[...], sin_ref[...]
    lane = lax.broadcasted_iota(jnp.int32, (ts, V7X_LANES), 1)
    for i, (use_rope, scale) in enumerate(specs):
        h = (xn * g_refs[i][...]).astype(jnp.bfloat16)
        for grp in range(N_DGROUPS):
            acc = jnp.dot(h, w_refs[i][:, grp * D_MODEL:(grp + 1) * D_MODEL], preferred_element_type=jnp.float32)
            slabs = []
            for s in range(ROW_SLABS):
                v = acc[:, s * V7X_LANES:(s + 1) * V7X_LANES]
                if use_rope:
                    v = _rope_slab(v, cos, sin, lane)
                if scale != 1.0:
                    v = v * scale
                slabs.append(v)
            for q in range(ROW_SLABS // 2):
                o_refs[i][grp, :, q * V7X_LANES:(q + 1) * V7X_LANES] = _pack_pair(slabs[2 * q], slabs[2 * q + 1])


def _qkv_proj(x2, moe, seq, projections):
    t, d = x2.shape
    ns = seq // SEQ_TILE
    n = len(projections)
    cos, sin = _rope_tables(seq)
    tok = pl.BlockSpec((SEQ_TILE, d), lambda i: (i, 0))
    rope_spec = pl.BlockSpec((SEQ_TILE, V7X_LANES), lambda i: (i % ns, 0))
    acts, act_specs, gather_scratch = [x2], [tok], []
    if moe is not None:
        g_args, g_specs, gather_scratch = _gather_operands(moe, t // SEQ_TILE, lambda i: i)
        acts += g_args
        act_specs += g_specs
    packed = jax.ShapeDtypeStruct((N_DGROUPS, t, d // 2), jnp.uint32)
    packed_spec = pl.BlockSpec((N_DGROUPS, SEQ_TILE, d // 2), lambda i: (0, i, 0))
    out_shape, out_specs = [packed] * n, [packed_spec] * n
    if moe is not None:
        out_shape = [jax.ShapeDtypeStruct((t, d), jnp.float32)] + out_shape
        out_specs = [tok] + out_specs
    specs = tuple((p[3], p[4]) for p in projections)
    w_spec = lambda col: pl.BlockSpec((d, N_DGROUPS * d), lambda i: (0, col), pipeline_mode=pl.Buffered(1))
    bf16_of = {}
    for p in projections:
        bf16_of.setdefault(id(p[1]), p[1].astype(jnp.bfloat16))
    return pl.pallas_call(
        functools.partial(_proj_kernel, moe is not None, specs),
        grid=(t // SEQ_TILE,),
        in_specs=act_specs + [rope_spec, rope_spec]
                 + [pl.BlockSpec((1, d), lambda i: (0, 0))] * n
                 + [w_spec(p[2]) for p in projections],
        out_specs=out_specs,
        out_shape=out_shape,
        scratch_shapes=gather_scratch,
        compiler_params=_cparams("arbitrary"),
        name="qkv_proj",
    )(*acts, cos, sin, *[_row(p[0]) for p in projections], *[bf16_of[id(p[1])] for p in projections])


def _attn_unit(pair, qw, kw, vw, bias, lane):
    unpack = _unpack_lo if pair == 0 else _unpack_hi
    q = unpack(qw).astype(jnp.bfloat16)
    k = unpack(kw).astype(jnp.bfloat16)
    v = unpack(vw).astype(jnp.bfloat16)
    first = lane < HEAD_DIM
    zero = jnp.zeros_like(q)
    qs = jnp.concatenate([jnp.where(first, q, zero), jnp.where(first, zero, q)], axis=0)
    s = lax.dot_general(qs, k, (((1,), (1,)), ((), ())), preferred_element_type=jnp.float32) + bias
    m = jnp.max(s, axis=1, keepdims=True)
    p = jnp.exp2(s - m)
    nq = ATT_BLOCK
    shape = (nq, V7X_LANES)
    pb = p.astype(jnp.bfloat16)
    if k.shape[0] > nq:
        pv = jnp.dot(pb, jnp.concatenate([v, jnp.ones_like(v)], axis=1), preferred_element_type=jnp.float32)
        l2 = jnp.where(first, pv[:nq, V7X_LANES:], pv[nq:, V7X_LANES:])
    else:
        pv = jnp.dot(pb, v, preferred_element_type=jnp.float32)
        l = jnp.sum(p, axis=1, keepdims=True)
        l2 = jnp.where(first, jnp.broadcast_to(l[:nq], shape), jnp.broadcast_to(l[nq:], shape))
    o = jnp.where(first, pv[:nq, :V7X_LANES], pv[nq:, :V7X_LANES])
    m2 = jnp.where(first, jnp.broadcast_to(m[:nq], shape), jnp.broadcast_to(m[nq:], shape))
    return o, m2, l2


def _attn_kernel(q_ref, k_ref, v_ref, o_ref, acc_ref, m_ref, l_ref):
    seq = q_ref.shape[1]
    nq = ATT_BLOCK
    lane = lax.broadcasted_iota(jnp.int32, (nq, V7X_LANES), 1)
    qi = lax.broadcasted_iota(jnp.int32, (2 * nq, 2 * nq), 0) % nq
    kj = lax.broadcasted_iota(jnp.int32, (2 * nq, 2 * nq), 1)
    band_prev = (kj < nq) & (kj >= qi)
    band_cur = (kj >= nq) & (kj - nq <= qi)
    bias_full = jnp.where(band_prev | band_cur, 0.0, NEG_BIG)
    bias_cur = bias_full[:, nq:]
    prev_cols = jnp.where(kj < nq, NEG_BIG, 0.0)

    order = sorted(range(N_DGROUPS), key=lambda g_: -DILATED_GROUPS[g_][1])
    for grp in order:
        dil = DILATED_GROUPS[grp][1]
        nblk = seq // dil // nq
        assert nblk % ATT_UNROLL == 0 or ATT_UNROLL % nblk == 0

        def rows(start):
            return pl.ds(pl.multiple_of(start, nq), nq) if dil == 1 else pl.ds(start, nq, stride=dil)

        def unit(it, u):
            idx = it * ATT_UNROLL + u
            n = u % nblk if ATT_UNROLL % nblk == 0 else idx % nblk
            r = idx // nblk
            start = n * nq * dil + r
            qw = q_ref[grp, rows(start), :]
            kw, vw = k_ref[grp, rows(start), :], v_ref[grp, rows(start), :]
            if isinstance(n, int) and n == 0:
                bias = bias_cur
            else:
                prev = jnp.maximum(n - 1, 0) * nq * dil + r
                kw = jnp.concatenate([k_ref[grp, rows(prev), :], kw], axis=0)
                vw = jnp.concatenate([v_ref[grp, rows(prev), :], vw], axis=0)
                bias = bias_full
                if not isinstance(n, int) and u == 0:
                    bias = bias + jnp.where(n > 0, 0.0, 1.0) * prev_cols
            for pair in range(2):
                o, m2, l2 = _attn_unit(pair, qw, kw, vw, bias, lane)
                if grp == order[0]:
                    acc_ref[pair, rows(start), :] = o
                    m_ref[pair, rows(start), :] = m2
                    l_ref[pair, rows(start), :] = l2
                else:
                    m_old = m_ref[pair, rows(start), :]
                    m_new = jnp.maximum(m_old, m2)
                    a_old, a_new = jnp.exp2(m_old - m_new), jnp.exp2(m2 - m_new)
                    acc_ref[pair, rows(start), :] = acc_ref[pair, rows(start), :] * a_old + o * a_new
                    l_ref[pair, rows(start), :] = l_ref[pair, rows(start), :] * a_old + l2 * a_new
                    m_ref[pair, rows(start), :] = m_new

        def units(it, carry):
            for u in range(ATT_UNROLL):
                unit(it, u)
            return carry

        lax.fori_loop(0, seq // nq // ATT_UNROLL, units, 0)

    for pair in range(2):
        o_ref[:, pair * V7X_LANES:(pair + 1) * V7X_LANES] = (acc_ref[pair] / l_ref[pair]).astype(o_ref.dtype)


def _dilated_attention(qp, kp, vp, batch):
    t = qp.shape[1]
    seq = t // batch
    nquad = qp.shape[2] // V7X_LANES
    spec = pl.BlockSpec((N_DGROUPS, seq, V7X_LANES), lambda b, c: (0, b, c))
    return pl.pallas_call(
        _attn_kernel,
        grid=(batch, nquad),
        in_specs=[spec, spec, spec],
        out_specs=pl.BlockSpec((seq, 2 * V7X_LANES), lambda b, c: (b, c)),
        out_shape=jax.ShapeDtypeStruct((t, D_MODEL), jnp.bfloat16),
        scratch_shapes=[pltpu.VMEM((2, seq, V7X_LANES), jnp.float32)] * 3,
        compiler_params=_cparams("arbitrary", "arbitrary"),
        name="dilated_attention",
    )(qp, kp, vp)


def _attn_out_kernel(x_ref, c_ref, w_ref, fg_ref, wr_ref, o_ref, hp_ref, cls_ref, rank_ref, cnt_ref, carry_ref):
    x_new = x_ref[...] + jnp.dot(c_ref[...], w_ref[...], preferred_element_type=jnp.float32)
    o_ref[...] = x_new
    _route_rows(x_new, pl.program_id(0) == 0, fg_ref, wr_ref, hp_ref, cls_ref, rank_ref, cnt_ref, carry_ref)


def _attn_out(x2, comb, w_o, router):
    t, d = x2.shape
    tok = pl.BlockSpec((SEQ_TILE, d), lambda i: (i, 0))
    r_args, r_specs, r_shape, r_out_specs, r_scratch = _route_operands(t, d, router, lambda i: i)
    x_new, *routed = pl.pallas_call(
        _attn_out_kernel,
        grid=(t // SEQ_TILE,),
        in_specs=[tok, tok, pl.BlockSpec((d, d), lambda i: (0, 0))] + r_specs,
        out_specs=[tok] + r_out_specs,
        out_shape=[jax.ShapeDtypeStruct((t, d), jnp.float32)] + r_shape,
        scratch_shapes=r_scratch,
        compiler_params=_cparams("arbitrary"),
        name="attn_out",
    )(x2, comb, w_o.astype(jnp.bfloat16), *r_args)
    return x_new, routed


def _final_kernel(x_ref, ys_ref, pos_ref, pos_next_ref, g_ref, o_ref, *gather_scratch):
    y = _gathered_rows(pl.program_id(0), pl.num_programs(0), ys_ref, pos_ref, pos_next_ref, *gather_scratch)
    o_ref[...] = _rms(x_ref[...] + y, g_ref[...])


def _final_norm(x2, moe, g):
    t, d = x2.shape
    tok = pl.BlockSpec((SEQ_TILE, d), lambda i: (i, 0))
    g_args, g_specs, gather_scratch = _gather_operands(moe, t // SEQ_TILE, lambda i: i)
    return pl.pallas_call(
        _final_kernel,
        grid=(t // SEQ_TILE,),
        in_specs=[tok] + g_specs + [pl.BlockSpec((1, d), lambda i: (0, 0))],
        out_specs=tok,
        out_shape=jax.ShapeDtypeStruct((t, d), jnp.float32),
        scratch_shapes=gather_scratch,
        compiler_params=_cparams("arbitrary"),
        name="final_norm",
    )(x2, *g_args, _row(g))


def kernel(x, a_norm, a_w_in, a_b_in, a_w_dw, a_b_dw, a_ln_g, a_ln_b, a_w_out, a_b_out, kv_norm, w_kv, b_norm,
           b_w_q, b_w_o, ffn_norm, router_group, router_expert, w_gate, w_up, w_down, final_norm):
    batch, seq, d = x.shape
    n_a = a_norm.shape[0]
    depth = ffn_norm.shape[0]
    q_scale = HEAD_DIM ** -0.5 * LOG2E
    x2 = x.reshape(batch * seq, d)
    expert_w = [w.reshape((depth * N_EXPERTS,) + w.shape[2:]).astype(jnp.bfloat16) for w in (w_gate, w_up, w_down)]
    moe = None
    kp = vp = None
    for l in range(depth):
        router = (ffn_norm[l], router_group[l], router_expert[l])
        if l < n_a:
            x2, routed = _a_layer(x2, moe, router, batch, a_norm[l], a_w_in[l], a_b_in[l], a_w_dw[l], a_b_dw[l],
                                  a_ln_g[l], a_ln_b[l], a_w_out[l], a_b_out[l])
        else:
            j = l - n_a
            projections = [(b_norm[j], b_w_q[j], 0, True, q_scale)]
            if kp is None:
                projections += [(kv_norm, w_kv, 0, True, 1.0), (kv_norm, w_kv, 1, False, 1.0)]
            outs = _qkv_proj(x2, moe, seq, projections)
            if moe is not None:
                x2, *outs = outs
            if kp is None:
                qp, kp, vp = outs
            else:
                qp, = outs
            comb = _dilated_attention(qp, kp, vp, batch)
            x2, routed = _attn_out(x2, comb, b_w_o[j], router)
        moe = _moe(routed, l, *expert_w)
    return _final_norm(x2, moe, final_norm).reshape(batch, seq, d)
```

```python
import functools
import math

import jax
import jax.numpy as jnp
import numpy as np
from jax import lax
from jax.experimental import pallas as pl
from jax.experimental.pallas import tpu as pltpu

D_MODEL = 1024
CONV_KERNEL = 31
HEAD_DIM = 64
N_HEADS = 16
DILATED_GROUPS = ((128, 1), (512, 4), (2048, 16))
N_DGROUPS = len(DILATED_GROUPS)
ATT_BLOCK = 128
ROPE_THETA = 10000.0
N_EXPERT_GROUPS = 4
EXPERTS_PER_GROUP = 4
N_EXPERTS = N_EXPERT_GROUPS * EXPERTS_PER_GROUP
D_EXPERT = D_MODEL // 2
EPS = 1e-6

V7X_LANES = 128
V7X_SUBLANES = 8
V7X_VMEM_LIMIT_BYTES = 56 * 1024 * 1024

SEQ_TILE = 512
CONV_HALO = 32
CONV_CHUNK = 32
ROW_SLABS = D_MODEL // V7X_LANES
EXPERT_TILE = 256
ATT_UNROLL = 8
DMA_THREADS = 2
LOG2E = math.log2(math.e)
N_PAIRS = EXPERTS_PER_GROUP * (EXPERTS_PER_GROUP - 1) // 2
N_CLASSES = N_EXPERT_GROUPS * N_PAIRS
CLASS_ROWS = 32
NEG_BIG = -1e30

_PAIRS = [(a, b) for a in range(EXPERTS_PER_GROUP) for b in range(a + 1, EXPERTS_PER_GROUP)]


def _cparams(*sem):
    return pltpu.CompilerParams(dimension_semantics=sem, vmem_limit_bytes=V7X_VMEM_LIMIT_BYTES)


def _rms(x, g):
    return x * lax.rsqrt(jnp.mean(x * x, axis=-1, keepdims=True) + EPS) * g


def _load_row8(ref, n):
    return jnp.concatenate([ref[pl.ds(s, n, stride=ROW_SLABS), :] for s in range(ROW_SLABS)], axis=1)


def _store_row8(ref, val, n):
    for s in range(ROW_SLABS):
        ref[pl.ds(s, n, stride=ROW_SLABS), :] = val[:, s * V7X_LANES:(s + 1) * V7X_LANES]


def _bf16_bits(x):
    return lax.bitcast_convert_type(x.astype(jnp.bfloat16).astype(jnp.float32), jnp.uint32)


def _pack_pair(lo, hi):
    return (_bf16_bits(lo) >> 16) | _bf16_bits(hi)


def _unpack_lo(w):
    return lax.bitcast_convert_type(w << 16, jnp.float32)


def _unpack_hi(w):
    return lax.bitcast_convert_type(w & jnp.uint32(0xFFFF0000), jnp.float32)


def _a_layer_kernel(has_y, *refs):
    if has_y:
        x_ref, ys_ref, pos_ref, pos_next_ref, *refs = refs
    else:
        x_ref, *refs = refs
    (g_ref, win_ref, bin_ref, wdw_ref, bdw_ref, lng_ref, lnb_ref, wout_ref, bout_ref, fg_ref, wr_ref,
     o_ref, hp_ref, cls_ref, rank_ref, cnt_ref, ext_ref, sh_ref, conv_ref, wb_ref, carry_ref,
     *gather_scratch) = refs
    ts, d = x_ref.shape
    step = pl.program_id(0) * pl.num_programs(1) + pl.program_id(1)

    @pl.when(pl.program_id(1) == 0)
    def _():
        ext_ref[0:CONV_HALO, :] = jnp.zeros((CONV_HALO, d), jnp.float32)

    x = x_ref[...]
    if has_y:
        n_steps = pl.num_programs(0) * pl.num_programs(1)
        x = x + _gathered_rows(step, n_steps, ys_ref, pos_ref, pos_next_ref, *gather_scratch)
    h = _rms(x, g_ref[...])
    u = jnp.dot(h.astype(jnp.bfloat16), win_ref[...], preferred_element_type=jnp.float32) + bin_ref[...]
    ext_ref[CONV_HALO:CONV_HALO + ts, :] = u[:, :d] * jax.nn.sigmoid(u[:, d:])

    base = CONV_HALO - (CONV_KERNEL - 1)
    nsh = ts + CONV_HALO - V7X_SUBLANES
    for j in range(1, V7X_SUBLANES):
        sh_ref[j - 1] = ext_ref[pl.ds(j, nsh), :]

    @pl.when(step == 0)
    def _():
        for k in range(CONV_KERNEL):
            wb_ref[k] = jnp.broadcast_to(wdw_ref[k:k + 1, :], (V7X_SUBLANES, d))
        wb_ref[CONV_KERNEL] = jnp.broadcast_to(bdw_ref[...], (V7X_SUBLANES, d))

    groups = CONV_CHUNK // V7X_SUBLANES

    def chunk(c, carry):
        r = pl.multiple_of(c * CONV_CHUNK, CONV_CHUNK)
        accs = [wb_ref[CONV_KERNEL]] * groups
        for k in range(CONV_KERNEL):
            q, j = divmod(base + k, V7X_SUBLANES)
            w = wb_ref[k]
            for gi in range(groups):
                rows = pl.ds(r + (q + gi) * V7X_SUBLANES, V7X_SUBLANES)
                tap = ext_ref[rows, :] if j == 0 else sh_ref[j - 1, rows, :]
                accs[gi] = accs[gi] + w * tap
        for gi in range(groups):
            conv_ref[pl.ds(r + gi * V7X_SUBLANES, V7X_SUBLANES), :] = accs[gi]
        return carry

    lax.fori_loop(0, ts // CONV_CHUNK, chunk, 0)
    ext_ref[0:CONV_HALO, :] = ext_ref[ts:ts + CONV_HALO, :]

    c = conv_ref[...]
    mu = jnp.mean(c, axis=-1, keepdims=True)
    cc = c - mu
    var = jnp.mean(cc * cc, axis=-1, keepdims=True)
    v = cc * lax.rsqrt(var + EPS) * lng_ref[...] + lnb_ref[...]
    v = v * jax.nn.sigmoid(v)
    out = jnp.dot(v.astype(jnp.bfloat16), wout_ref[...], preferred_element_type=jnp.float32)
    x_new = x + out + bout_ref[...]
    o_ref[...] = x_new
    _route_rows(x_new, step == 0, fg_ref, wr_ref, hp_ref, cls_ref, rank_ref, cnt_ref, carry_ref)


def _row(v):
    return v.reshape(1, -1)


def _a_layer(x2, moe, router, batch, g, w_in, b_in, w_dw, b_dw, ln_g, ln_b, w_out, b_out):
    t, d = x2.shape
    ns = t // batch // SEQ_TILE
    step_of = lambda b, s: b * ns + s
    const = lambda shape: pl.BlockSpec(shape, lambda b, s: (0, 0))
    weight = lambda shape: pl.BlockSpec(shape, lambda b, s: (0, 0), pipeline_mode=pl.Buffered(1))
    tok = pl.BlockSpec((SEQ_TILE, d), lambda b, s: (step_of(b, s), 0))
    acts, act_specs, gather_scratch = [x2], [tok], []
    if moe is not None:
        g_args, g_specs, gather_scratch = _gather_operands(moe, batch * ns, step_of)
        acts += g_args
        act_specs += g_specs
    r_args, r_specs, r_shape, r_out_specs, r_scratch = _route_operands(t, d, router, step_of)
    x_new, *routed = pl.pallas_call(
        functools.partial(_a_layer_kernel, moe is not None),
        grid=(batch, ns),
        in_specs=act_specs + [
            const((1, d)), weight((d, 2 * d)), const((1, 2 * d)), const((CONV_KERNEL, d)), const((1, d)),
            const((1, d)), const((1, d)), weight((d, d)), const((1, d)),
        ] + r_specs,
        out_specs=[tok] + r_out_specs,
        out_shape=[jax.ShapeDtypeStruct((t, d), jnp.float32)] + r_shape,
        scratch_shapes=[pltpu.VMEM((CONV_HALO + SEQ_TILE, d), jnp.float32),
                        pltpu.VMEM((V7X_SUBLANES - 1, CONV_HALO + SEQ_TILE - V7X_SUBLANES, d), jnp.float32),
                        pltpu.VMEM((SEQ_TILE, d), jnp.float32),
                        pltpu.VMEM((CONV_KERNEL + 1, V7X_SUBLANES, d), jnp.float32)] + r_scratch + gather_scratch,
        compiler_params=_cparams("arbitrary", "arbitrary"),
        name="a_layer",
    )(*acts, _row(g), w_in.astype(jnp.bfloat16), _row(b_in), w_dw, _row(b_dw), _row(ln_g), _row(ln_b),
      w_out.astype(jnp.bfloat16), _row(b_out), *r_args)
    return x_new, routed


def _route_rows(x, first_step, g_ref, wr_ref, hp_ref, cls_ref, rank_ref, cnt_ref, carry_ref):
    tr = x.shape[0]

    @pl.when(first_step)
    def _():
        carry_ref[...] = jnp.zeros_like(carry_ref)

    h = _rms(x, g_ref[...])
    lt = lax.dot_general(wr_ref[...], h.astype(jnp.bfloat16), (((1,), (1,)), ((), ())),
                         preferred_element_type=jnp.float32)
    lg = [lt[i:i + 1, :] for i in range(N_EXPERT_GROUPS)]
    gmax = functools.reduce(jnp.maximum, lg)
    g_idx = jnp.full_like(gmax, N_EXPERT_GROUPS - 1, dtype=jnp.int32)
    for i in reversed(range(N_EXPERT_GROUPS - 1)):
        g_idx = jnp.where(lg[i] == gmax, i, g_idx)
    p_top = 1.0 / functools.reduce(jnp.add, [jnp.exp(v - gmax) for v in lg])
    sel = []
    for e in range(EXPERTS_PER_GROUP):
        v = lt[N_EXPERT_GROUPS + e:N_EXPERT_GROUPS + e + 1, :]
        for g in range(1, N_EXPERT_GROUPS):
            r = N_EXPERT_GROUPS + g * EXPERTS_PER_GROUP + e
            v = jnp.where(g_idx == g, lt[r:r + 1, :], v)
        sel.append(v)
    v1 = functools.reduce(jnp.maximum, sel)
    i1 = jnp.full_like(g_idx, EXPERTS_PER_GROUP - 1)
    for e in reversed(range(EXPERTS_PER_GROUP - 1)):
        i1 = jnp.where(sel[e] == v1, e, i1)
    rest = [jnp.where(i1 == e, -jnp.inf, sel[e]) for e in range(EXPERTS_PER_GROUP)]
    v2 = functools.reduce(jnp.maximum, rest)
    i2 = jnp.full_like(g_idx, EXPERTS_PER_GROUP - 1)
    for e in reversed(range(EXPERTS_PER_GROUP - 1)):
        i2 = jnp.where((rest[e] == v2) & (i1 != e), e, i2)
    i2 = jnp.where((i2 == i1), jnp.where(i1 == EXPERTS_PER_GROUP - 1, EXPERTS_PER_GROUP - 2, i2), i2)
    t2 = jnp.exp(v2 - v1)
    w1 = p_top / (1.0 + t2)
    w2 = p_top * t2 / (1.0 + t2)
    first_lo = i1 < i2
    e_lo = jnp.where(first_lo, i1, i2)
    e_hi = jnp.where(first_lo, i2, i1)
    w_lo = jnp.where(first_lo, w1, w2)
    w_hi = jnp.where(first_lo, w2, w1)
    pair_base = jnp.where(e_lo == 0, 0, jnp.where(e_lo == 1, EXPERTS_PER_GROUP - 1, 2 * EXPERTS_PER_GROUP - 3))
    cls = g_idx * N_PAIRS + pair_base + (e_hi - e_lo - 1)

    onehot = (lax.broadcasted_iota(jnp.int32, (CLASS_ROWS, tr), 0) == cls)
    before = (lax.broadcasted_iota(jnp.int32, (tr, tr), 0) < lax.broadcasted_iota(jnp.int32, (tr, tr), 1))
    cum = jnp.dot(onehot.astype(jnp.bfloat16), before.astype(jnp.bfloat16), preferred_element_type=jnp.float32)
    oh = onehot.astype(jnp.float32)
    rank = jnp.sum(oh * (cum + carry_ref[...]), axis=0, keepdims=True)
    carry_ref[...] = carry_ref[...] + jnp.sum(oh, axis=1, keepdims=True)
    cls_ref[0] = cls
    rank_ref[0] = rank.astype(jnp.int32)
    cnt_ref[...] = jnp.broadcast_to(carry_ref[...], cnt_ref.shape)

    half = D_MODEL // 2
    word = _pack_pair(h[:, :half], h[:, half:])
    n_word = half // V7X_LANES
    for s in range(n_word):
        hp_ref[pl.ds(s, tr, stride=ROW_SLABS), :] = word[:, s * V7X_LANES:(s + 1) * V7X_LANES]
    wcols = jnp.concatenate([w_lo, w_hi, jnp.zeros((V7X_LANES - 2, tr), jnp.float32)], axis=0).T
    hp_ref[pl.ds(n_word, tr, stride=ROW_SLABS), :] = lax.bitcast_convert_type(wcols, jnp.uint32)
    for s in range(n_word + 1, ROW_SLABS):
        hp_ref[pl.ds(s, tr, stride=ROW_SLABS), :] = jnp.zeros((tr, V7X_LANES), jnp.uint32)


def _route_operands(t, d, router, step_of):
    g, r_grp, r_exp = router
    nt = t // SEQ_TILE
    wr = jnp.concatenate([r_grp, jnp.transpose(r_exp, (1, 0, 2)).reshape(d, N_EXPERTS)], axis=1)
    wr = jnp.pad(wr, ((0, 0), (0, CLASS_ROWS - wr.shape[1]))).T.astype(jnp.bfloat16)
    const = lambda shape: pl.BlockSpec(shape, lambda *g_: (0, 0))
    vec = lambda dt: jax.ShapeDtypeStruct((nt, 1, SEQ_TILE), dt)
    vec_spec = pl.BlockSpec((1, 1, SEQ_TILE), lambda *g_: (step_of(*g_), 0, 0))
    return ([_row(g), wr], [const((1, d)), const((CLASS_ROWS, d))],
            [jax.ShapeDtypeStruct((t * ROW_SLABS, V7X_LANES), jnp.uint32), vec(jnp.int32), vec(jnp.int32),
             jax.ShapeDtypeStruct((CLASS_ROWS, V7X_LANES), jnp.float32)],
            [pl.BlockSpec((SEQ_TILE * ROW_SLABS, V7X_LANES), lambda *g_: (step_of(*g_), 0)), vec_spec, vec_spec,
             const((CLASS_ROWS, V7X_LANES))],
            [pltpu.VMEM((CLASS_ROWS, 1), jnp.float32)])


def _row_slice(ref, row):
    return ref.at[pl.ds(pl.multiple_of(row * ROW_SLABS, ROW_SLABS), ROW_SLABS)]


def _scatter_kernel(pos_ref, pad_ref, src_ref, dst_ref, zero_ref, sem, zero_sem):
    n = pos_ref.shape[2]
    tile_rows = EXPERT_TILE * ROW_SLABS

    def zero_copy(row):
        return pltpu.make_async_copy(zero_ref.at[pl.ds(0, ROW_SLABS)], _row_slice(dst_ref, row), zero_sem)

    def zero_tile_copy(j):
        return pltpu.make_async_copy(zero_ref, dst_ref.at[pl.ds(pl.multiple_of(j * tile_rows, tile_rows), tile_rows)],
                                     zero_sem)

    def for_each_pad_row(fn):
        for c in range(N_CLASSES):
            first = pad_ref[0, 0, c]
            lax.fori_loop(0, pad_ref[0, 1, c], lambda k, carry: fn(first + k) or carry, 0)

    def for_each_unused_tile(fn):
        lax.fori_loop(pad_ref[0, 0, N_CLASSES], dst_ref.shape[0] // tile_rows, lambda j, carry: fn(j) or carry, 0)

    @pl.when(pl.program_id(0) == 0)
    def _():
        zero_ref[...] = jnp.zeros_like(zero_ref)
        for_each_pad_row(lambda row: zero_copy(row).start())
        for_each_unused_tile(lambda j: zero_tile_copy(j).start())

    def issue(i, carry):
        for u in range(DMA_THREADS):
            k = i * DMA_THREADS + u
            pltpu.make_async_copy(_row_slice(src_ref, k), _row_slice(dst_ref, pos_ref[0, 0, k]),
                                  sem).start(priority=u)
        return carry

    lax.fori_loop(0, n // DMA_THREADS, issue, 0)
    pltpu.make_async_copy(src_ref, dst_ref.at[pl.ds(0, n * ROW_SLABS)], sem).wait()

    @pl.when(pl.program_id(0) == 0)
    def _():
        for_each_pad_row(lambda row: zero_copy(row).wait())
        for_each_unused_tile(lambda j: zero_tile_copy(j).wait())


def _row_scatter(src, pos3, pad, n_dst_rows):
    nt, _, n = pos3.shape
    return pl.pallas_call(
        _scatter_kernel,
        grid=(nt,),
        in_specs=[pl.BlockSpec((1, 1, n), lambda i: (i, 0, 0), memory_space=pltpu.SMEM),
                  pl.BlockSpec(pad.shape, lambda i: (0, 0, 0), memory_space=pltpu.SMEM),
                  pl.BlockSpec((n * ROW_SLABS, V7X_LANES), lambda i: (i, 0))],
        out_specs=pl.BlockSpec(memory_space=pl.ANY),
        out_shape=jax.ShapeDtypeStruct((n_dst_rows * ROW_SLABS, V7X_LANES), src.dtype),
        scratch_shapes=[pltpu.VMEM((EXPERT_TILE * ROW_SLABS, V7X_LANES), src.dtype), pltpu.SemaphoreType.DMA,
                        pltpu.SemaphoreType.DMA],
        compiler_params=_cparams("arbitrary"),
        name="row_scatter",
    )(pos3, pad, src)


def _gather_copy(ys_ref, pos_ref, buf_ref, sem, slot, k):
    return pltpu.make_async_copy(_row_slice(ys_ref, pos_ref[0, 0, k]), _row_slice(buf_ref.at[slot], k), sem.at[slot])


def _gathered_rows(step, n_steps, ys_ref, pos_ref, pos_next_ref, buf_ref, sem):
    slot = step % 2
    n_tok = pos_ref.shape[2]

    @pl.when(step == 0)
    def _():
        def issue(k, carry):
            _gather_copy(ys_ref, pos_ref, buf_ref, sem, 0, k).start()
            return carry

        lax.fori_loop(0, n_tok, issue, 0)

    def wait(s):
        pltpu.make_async_copy(ys_ref.at[pl.ds(0, n_tok * ROW_SLABS)], buf_ref.at[s], sem.at[s]).wait()

    wait(slot)
    y = _load_row8(buf_ref.at[slot], n_tok)
    for k in range(n_tok):
        _gather_copy(ys_ref, pos_next_ref, buf_ref, sem, 1 - slot, k).start(priority=k % DMA_THREADS)

    @pl.when(step == n_steps - 1)
    def _():
        wait(1 - slot)

    return y


def _gather_operands(moe, n_steps, step_of):
    ys, pos3 = moe
    n = pos3.shape[2]
    smem = lambda m: pl.BlockSpec((1, 1, n), m, memory_space=pltpu.SMEM)
    cur = lambda *g: (step_of(*g), 0, 0)
    nxt = lambda *g: (jnp.minimum(step_of(*g) + 1, n_steps - 1), 0, 0)
    return ([ys, pos3, pos3], [pl.BlockSpec(memory_space=pl.ANY), smem(cur), smem(nxt)],
            [pltpu.VMEM((2, n * ROW_SLABS, V7X_LANES), jnp.float32), pltpu.SemaphoreType.DMA((2,))])


def _expert_kernel(lo_ref, hi_ref, nt_ref, hs_ref, wg_lo, wu_lo, wd_lo, wg_hi, wu_hi, wd_hi, y_ref):
    tm = EXPERT_TILE
    j = pl.program_id(0)

    @pl.when(j < nt_ref[0])
    def _():
        n_word = D_MODEL // 2 // V7X_LANES
        words = [hs_ref[pl.ds(s, tm, stride=ROW_SLABS), :] for s in range(n_word)]
        h = jnp.concatenate([_unpack_lo(w) for w in words] + [_unpack_hi(w) for w in words], axis=1)
        h = h.astype(jnp.bfloat16)
        gates = lax.bitcast_convert_type(hs_ref[pl.ds(n_word, tm, stride=ROW_SLABS), :], jnp.float32)
        y = None
        for col, (wg, wu, wd) in enumerate(((wg_lo, wu_lo, wd_lo), (wg_hi, wu_hi, wd_hi))):
            a = jnp.dot(h, wg[0], preferred_element_type=jnp.float32)
            u = jnp.dot(h, wu[0], preferred_element_type=jnp.float32)
            hid = a * jax.nn.sigmoid(a) * u * gates[:, col:col + 1]
            part = jnp.dot(hid.astype(jnp.bfloat16), wd[0], preferred_element_type=jnp.float32)
            y = part if y is None else y + part
        _store_row8(y_ref, y, tm)

    @pl.when(j >= nt_ref[0])
    def _():
        y_ref[...] = jnp.zeros_like(y_ref)


def _moe_experts(hs, tile_lo, tile_hi, n_tiles, w_gate, w_up, w_down):
    n_rows = hs.shape[0] // ROW_SLABS
    nt_max = n_rows // EXPERT_TILE
    d, f = w_gate.shape[1], w_gate.shape[2]
    act = lambda j, lo, hi, nt: (jnp.maximum(jnp.minimum(j, nt[0] - 1), 0), 0)
    w_lo = lambda j, lo, hi, nt: (lo[j], 0, 0)
    w_hi = lambda j, lo, hi, nt: (hi[j], 0, 0)
    up_spec = lambda m: pl.BlockSpec((1, d, f), m)
    dn_spec = lambda m: pl.BlockSpec((1, f, d), m)
    blk = (EXPERT_TILE * ROW_SLABS, V7X_LANES)
    wg, wu, wd = w_gate, w_up, w_down
    return pl.pallas_call(
        _expert_kernel,
        grid_spec=pltpu.PrefetchScalarGridSpec(
            num_scalar_prefetch=3,
            grid=(nt_max,),
            in_specs=[pl.BlockSpec(blk, act), up_spec(w_lo), up_spec(w_lo), dn_spec(w_lo),
                      up_spec(w_hi), up_spec(w_hi), dn_spec(w_hi)],
            out_specs=pl.BlockSpec(blk, lambda j, lo, hi, nt: (j, 0)),
        ),
        out_shape=jax.ShapeDtypeStruct((n_rows * ROW_SLABS, V7X_LANES), jnp.float32),
        compiler_params=_cparams("arbitrary"),
        name="moe_experts",
    )(tile_lo, tile_hi, n_tiles, hs, wg, wu, wd, wg, wu, wd)


def _moe(routed, layer, w_gate, w_up, w_down):
    hp, cls, rank, cnt = routed
    t = hp.shape[0] // ROW_SLABS
    counts = cnt[:N_CLASSES, 0].astype(jnp.int32)
    tiles_per = (counts + EXPERT_TILE - 1) // EXPERT_TILE
    tile_end = jnp.cumsum(tiles_per)
    base = (tile_end - tiles_per) * EXPERT_TILE
    cls = cls.reshape(t)
    pos = jnp.sum(jnp.where(cls[:, None] == jnp.arange(N_CLASSES)[None, :], base[None, :], 0), axis=1)
    pos = (pos + rank.reshape(t)).astype(jnp.int32)
    nt_max = t // EXPERT_TILE + N_CLASSES
    tile_cls = jnp.sum(jnp.arange(nt_max)[:, None] >= tile_end[None, :], axis=1)
    tile_cls = jnp.minimum(tile_cls, N_CLASSES - 1)
    lo_tab = jnp.array([p[0] for p in _PAIRS], jnp.int32)
    hi_tab = jnp.array([p[1] for p in _PAIRS], jnp.int32)
    tile_g = tile_cls // N_PAIRS + layer * N_EXPERT_GROUPS
    tile_lo = (tile_g * EXPERTS_PER_GROUP + lo_tab[tile_cls % N_PAIRS]).astype(jnp.int32)
    tile_hi = (tile_g * EXPERTS_PER_GROUP + hi_tab[tile_cls % N_PAIRS]).astype(jnp.int32)
    n_tiles = tile_end[-1:].astype(jnp.int32)
    n_sorted = nt_max * EXPERT_TILE
    pos3 = pos.reshape(t // SEQ_TILE, 1, SEQ_TILE)
    pad = jnp.stack([base + counts, tiles_per * EXPERT_TILE - counts]).astype(jnp.int32)
    pad = jnp.pad(pad, ((0, 0), (0, CLASS_ROWS - N_CLASSES)))
    pad = pad.at[0, N_CLASSES].set(n_tiles[0])[None]
    hs = _row_scatter(hp, pos3, pad, n_sorted)
    ys = _moe_experts(hs, tile_lo, tile_hi, n_tiles, w_gate, w_up, w_down)
    return ys, pos3


def _rope_tables(seq):
    pos = jnp.arange(seq, dtype=jnp.float32)
    inv_freq = ROPE_THETA ** (-jnp.arange(0, HEAD_DIM, 2, dtype=jnp.float32) / HEAD_DIM)
    ang = pos[:, None] * inv_freq[None, :]
    reps = V7X_LANES // (HEAD_DIM // 2)
    cos = jnp.tile(jnp.cos(ang), (1, reps))
    sign = jnp.tile(jnp.concatenate([-jnp.ones(HEAD_DIM // 2), jnp.ones(HEAD_DIM // 2)]), V7X_LANES // HEAD_DIM)
    sin = jnp.tile(jnp.sin(ang), (1, reps)) * sign[None, :]
    return cos, sin


def _rope_slab(v, cos, sin, lane):
    half = HEAD_DIM // 2
    partner = jnp.where((lane % HEAD_DIM) < half, pltpu.roll(v, V7X_LANES - half, axis=1), pltpu.roll(v, half, axis=1))
    return v * cos + partner * sin


def _proj_kernel(has_y, specs, *refs):
    n = len(specs)
    if has_y:
        x_ref, ys_ref, pos_ref, pos_next_ref, *refs = refs
    else:
        x_ref, *refs = refs
    cos_ref, sin_ref, *refs = refs
    g_refs, w_refs, refs = refs[:n], refs[n:2 * n], refs[2 * n:]
    if has_y:
        xo_ref, *refs = refs
    o_refs, gather_scratch = refs[:n], refs[n:]
    ts = x_ref.shape[0]
    x = x_ref[...]
    if has_y:
        x = x + _gathered_rows(pl.program_id(0), pl.num_programs(0), ys_ref, pos_ref, pos_next_ref, *gather_scratch)
        xo_ref[...] = x
    inv = lax.rsqrt(jnp.mean(x * x, axis=-1, keepdims=True) + EPS)
    xn = x * inv
    cos, sin = cos_ref[...], sin_ref[...]
    lane = lax.broadcasted_iota(jnp.int32, (ts, V7X_LANES), 1)
    for i, (use_rope, scale) in enumerate(specs):
        h = (xn * g_refs[i][...]).astype(jnp.bfloat16)
        for grp in range(N_DGROUPS):
            acc = jnp.dot(h, w_refs[i][:, grp * D_MODEL:(grp + 1) * D_MODEL], preferred_element_type=jnp.float32)
            slabs = []
            for s in range(ROW_SLABS):
                v = acc[:, s * V7X_LANES:(s + 1) * V7X_LANES]
                if use_rope:
                    v = _rope_slab(v, cos, sin, lane)
                if scale != 1.0:
                    v = v * scale
                slabs.append(v)
            for q in range(ROW_SLABS // 2):
                o_refs[i][grp, :, q * V7X_LANES:(q + 1) * V7X_LANES] = _pack_pair(slabs[2 * q], slabs[2 * q + 1])


def _qkv_proj(x2, moe, seq, projections):
    t, d = x2.shape
    ns = seq // SEQ_TILE
    n = len(projections)
    cos, sin = _rope_tables(seq)
    tok = pl.BlockSpec((SEQ_TILE, d), lambda i: (i, 0))
    rope_spec = pl.BlockSpec((SEQ_TILE, V7X_LANES), lambda i: (i % ns, 0))
    acts, act_specs, gather_scratch = [x2], [tok], []
    if moe is not None:
        g_args, g_specs, gather_scratch = _gather_operands(moe, t // SEQ_TILE, lambda i: i)
        acts += g_args
        act_specs += g_specs
    packed = jax.ShapeDtypeStruct((N_DGROUPS, t, d // 2), jnp.uint32)
    packed_spec = pl.BlockSpec((N_DGROUPS, SEQ_TILE, d // 2), lambda i: (0, i, 0))
    out_shape, out_specs = [packed] * n, [packed_spec] * n
    if moe is not None:
        out_shape = [jax.ShapeDtypeStruct((t, d), jnp.float32)] + out_shape
        out_specs = [tok] + out_specs
    specs = tuple((p[3], p[4]) for p in projections)
    w_spec = lambda col: pl.BlockSpec((d, N_DGROUPS * d), lambda i: (0, col), pipeline_mode=pl.Buffered(1))
    bf16_of = {}
    for p in projections:
        bf16_of.setdefault(id(p[1]), p[1].astype(jnp.bfloat16))
    return pl.pallas_call(
        functools.partial(_proj_kernel, moe is not None, specs),
        grid=(t // SEQ_TILE,),
        in_specs=act_specs + [rope_spec, rope_spec]
                 + [pl.BlockSpec((1, d), lambda i: (0, 0))] * n
                 + [w_spec(p[2]) for p in projections],
        out_specs=out_specs,
        out_shape=out_shape,
        scratch_shapes=gather_scratch,
        compiler_params=_cparams("arbitrary"),
        name="qkv_proj",
    )(*acts, cos, sin, *[_row(p[0]) for p in projections], *[bf16_of[id(p[1])] for p in projections])


def _attn_unit(pair, qw, kw, vw, bias, lane):
    unpack = _unpack_lo if pair == 0 else _unpack_hi
    q = unpack(qw).astype(jnp.bfloat16)
    k = unpack(kw).astype(jnp.bfloat16)
    v = unpack(vw).astype(jnp.bfloat16)
    first = lane < HEAD_DIM
    zero = jnp.zeros_like(q)
    qs = jnp.concatenate([jnp.where(first, q, zero), jnp.where(first, zero, q)], axis=0)
    s = lax.dot_general(qs, k, (((1,), (1,)), ((), ())), preferred_element_type=jnp.float32) + bias
    m = jnp.max(s, axis=1, keepdims=True)
    p = jnp.exp2(s - m)
    nq = ATT_BLOCK
    shape = (nq, V7X_LANES)
    pb = p.astype(jnp.bfloat16)
    if k.shape[0] > nq:
        pv = jnp.dot(pb, jnp.concatenate([v, jnp.ones_like(v)], axis=1), preferred_element_type=jnp.float32)
        l2 = jnp.where(first, pv[:nq, V7X_LANES:], pv[nq:, V7X_LANES:])
    else:
        pv = jnp.dot(pb, v, preferred_element_type=jnp.float32)
        l = jnp.sum(p, axis=1, keepdims=True)
        l2 = jnp.where(first, jnp.broadcast_to(l[:nq], shape), jnp.broadcast_to(l[nq:], shape))
    o = jnp.where(first, pv[:nq, :V7X_LANES], pv[nq:, :V7X_LANES])
    m2 = jnp.where(first, jnp.broadcast_to(m[:nq], shape), jnp.broadcast_to(m[nq:], shape))
    return o, m2, l2


def _attn_kernel(q_ref, k_ref, v_ref, o_ref, acc_ref, m_ref, l_ref):
    seq = q_ref.shape[1]
    nq = ATT_BLOCK
    lane = lax.broadcasted_iota(jnp.int32, (nq, V7X_LANES), 1)
    qi = lax.broadcasted_iota(jnp.int32, (2 * nq, 2 * nq), 0) % nq
    kj = lax.broadcasted_iota(jnp.int32, (2 * nq, 2 * nq), 1)
    band_prev = (kj < nq) & (kj >= qi)
    band_cur = (kj >= nq) & (kj - nq <= qi)
    bias_full = jnp.where(band_prev | band_cur, 0.0, NEG_BIG)
    bias_cur = bias_full[:, nq:]
    prev_cols = jnp.where(kj < nq, NEG_BIG, 0.0)

    order = sorted(range(N_DGROUPS), key=lambda g_: -DILATED_GROUPS[g_][1])
    for grp in order:
        dil = DILATED_GROUPS[grp][1]
        nblk = seq // dil // nq
        assert nblk % ATT_UNROLL == 0 or ATT_UNROLL % nblk == 0

        def rows(start):
            return pl.ds(pl.multiple_of(start, nq), nq) if dil == 1 else pl.ds(start, nq, stride=dil)

        def unit(it, u):
            idx = it * ATT_UNROLL + u
            n = u % nblk if ATT_UNROLL % nblk == 0 else idx % nblk
            r = idx // nblk
            start = n * nq * dil + r
            qw = q_ref[grp, rows(start), :]
            kw, vw = k_ref[grp, rows(start), :], v_ref[grp, rows(start), :]
            if isinstance(n, int) and n == 0:
                bias = bias_cur
            else:
                prev = jnp.maximum(n - 1, 0) * nq * dil + r
                kw = jnp.concatenate([k_ref[grp, rows(prev), :], kw], axis=0)
                vw = jnp.concatenate([v_ref[grp, rows(prev), :], vw], axis=0)
                bias = bias_full
                if not isinstance(n, int) and u == 0:
                    bias = bias + jnp.where(n > 0, 0.0, 1.0) * prev_cols
            for pair in range(2):
                o, m2, l2 = _attn_unit(pair, qw, kw, vw, bias, lane)
                if grp == order[0]:
                    acc_ref[pair, rows(start), :] = o
                    m_ref[pair, rows(start), :] = m2
                    l_ref[pair, rows(start), :] = l2
                else:
                    m_old = m_ref[pair, rows(start), :]
                    m_new = jnp.maximum(m_old, m2)
                    a_old, a_new = jnp.exp2(m_old - m_new), jnp.exp2(m2 - m_new)
                    acc_ref[pair, rows(start), :] = acc_ref[pair, rows(start), :] * a_old + o * a_new
                    l_ref[pair, rows(start), :] = l_ref[pair, rows(start), :] * a_old + l2 * a_new
                    m_ref[pair, rows(start), :] = m_new

        def units(it, carry):
            for u in range(ATT_UNROLL):
                unit(it, u)
            return carry

        lax.fori_loop(0, seq // nq // ATT_UNROLL, units, 0)

    for pair in range(2):
        o_ref[:, pair * V7X_LANES:(pair + 1) * V7X_LANES] = (acc_ref[pair] / l_ref[pair]).astype(o_ref.dtype)


def _dilated_attention(qp, kp, vp, batch):
    t = qp.shape[1]
    seq = t // batch
    nquad = qp.shape[2] // V7X_LANES
    spec = pl.BlockSpec((N_DGROUPS, seq, V7X_LANES), lambda b, c: (0, b, c))
    return pl.pallas_call(
        _attn_kernel,
        grid=(batch, nquad),
        in_specs=[spec, spec, spec],
        out_specs=pl.BlockSpec((seq, 2 * V7X_LANES), lambda b, c: (b, c)),
        out_shape=jax.ShapeDtypeStruct((t, D_MODEL), jnp.bfloat16),
        scratch_shapes=[pltpu.VMEM((2, seq, V7X_LANES), jnp.float32)] * 3,
        compiler_params=_cparams("arbitrary", "arbitrary"),
        name="dilated_attention",
    )(qp, kp, vp)


def _attn_out_kernel(x_ref, c_ref, w_ref, fg_ref, wr_ref, o_ref, hp_ref, cls_ref, rank_ref, cnt_ref, carry_ref):
    x_new = x_ref[...] + jnp.dot(c_ref[...], w_ref[...], preferred_element_type=jnp.float32)
    o_ref[...] = x_new
    _route_rows(x_new, pl.program_id(0) == 0, fg_ref, wr_ref, hp_ref, cls_ref, rank_ref, cnt_ref, carry_ref)


def _attn_out(x2, comb, w_o, router):
    t, d = x2.shape
    tok = pl.BlockSpec((SEQ_TILE, d), lambda i: (i, 0))
    r_args, r_specs, r_shape, r_out_specs, r_scratch = _route_operands(t, d, router, lambda i: i)
    x_new, *routed = pl.pallas_call(
        _attn_out_kernel,
        grid=(t // SEQ_TILE,),
        in_specs=[tok, tok, pl.BlockSpec((d, d), lambda i: (0, 0))] + r_specs,
        out_specs=[tok] + r_out_specs,
        out_shape=[jax.ShapeDtypeStruct((t, d), jnp.float32)] + r_shape,
        scratch_shapes=r_scratch,
        compiler_params=_cparams("arbitrary"),
        name="attn_out",
    )(x2, comb, w_o.astype(jnp.bfloat16), *r_args)
    return x_new, routed


def _final_kernel(x_ref, ys_ref, pos_ref, pos_next_ref, g_ref, o_ref, *gather_scratch):
    y = _gathered_rows(pl.program_id(0), pl.num_programs(0), ys_ref, pos_ref, pos_next_ref, *gather_scratch)
    o_ref[...] = _rms(x_ref[...] + y, g_ref[...])


def _final_norm(x2, moe, g):
    t, d = x2.shape
    tok = pl.BlockSpec((SEQ_TILE, d), lambda i: (i, 0))
    g_args, g_specs, gather_scratch = _gather_operands(moe, t // SEQ_TILE, lambda i: i)
    return pl.pallas_call(
        _final_kernel,
        grid=(t // SEQ_TILE,),
        in_specs=[tok] + g_specs + [pl.BlockSpec((1, d), lambda i: (0, 0))],
        out_specs=tok,
        out_shape=jax.ShapeDtypeStruct((t, d), jnp.float32),
        scratch_shapes=gather_scratch,
        compiler_params=_cparams("arbitrary"),
        name="final_norm",
    )(x2, *g_args, _row(g))


def kernel(x, a_norm, a_w_in, a_b_in, a_w_dw, a_b_dw, a_ln_g, a_ln_b, a_w_out, a_b_out, kv_norm, w_kv, b_norm,
           b_w_q, b_w_o, ffn_norm, router_group, router_expert, w_gate, w_up, w_down, final_norm):
    batch, seq, d = x.shape
    n_a = a_norm.shape[0]
    depth = ffn_norm.shape[0]
    q_scale = HEAD_DIM ** -0.5 * LOG2E
    x2 = x.reshape(batch * seq, d)
    expert_w = [w.reshape((depth * N_EXPERTS,) + w.shape[2:]).astype(jnp.bfloat16) for w in (w_gate, w_up, w_down)]
    moe = None
    kp = vp = None
    for l in range(depth):
        router = (ffn_norm[l], router_group[l], router_expert[l])
        if l < n_a:
            x2, routed = _a_layer(x2, moe, router, batch, a_norm[l], a_w_in[l], a_b_in[l], a_w_dw[l], a_b_dw[l],
                                  a_ln_g[l], a_ln_b[l], a_w_out[l], a_b_out[l])
        else:
            j = l - n_a
            projections = [(b_norm[j], b_w_q[j], 0, True, q_scale)]
            if kp is None:
                projections += [(kv_norm, w_kv, 0, True, 1.0), (kv_norm, w_kv, 1, False, 1.0)]
            outs = _qkv_proj(x2, moe, seq, projections)
            if moe is not None:
                x2, *outs = outs
            if kp is None:
                qp, kp, vp = outs
            else:
                qp, = outs
            comb = _dilated_attention(qp, kp, vp, batch)
            x2, routed = _attn_out(x2, comb, b_w_o[j], router)
        moe = _moe(routed, l, *expert_w)
    return _final_norm(x2, moe, final_norm).reshape(batch, seq, d)
```

```python
import functools
import math

import jax
import jax.numpy as jnp
import numpy as np
from jax import lax
from jax.experimental import pallas as pl
from jax.experimental.pallas import tpu as pltpu

D_MODEL = 1024
CONV_KERNEL = 31
HEAD_DIM = 64
N_HEADS = 16
DILATED_GROUPS = ((128, 1), (512, 4), (2048, 16))
N_DGROUPS = len(DILATED_GROUPS)
ATT_BLOCK = 128
ROPE_THETA = 10000.0
N_EXPERT_GROUPS = 4
EXPERTS_PER_GROUP = 4
N_EXPERTS = N_EXPERT_GROUPS * EXPERTS_PER_GROUP
D_EXPERT = D_MODEL // 2
EPS = 1e-6

V7X_LANES = 128
V7X_SUBLANES = 8
V7X_VMEM_LIMIT_BYTES = 56 * 1024 * 1024

SEQ_TILE = 512
CONV_HALO = 32
CONV_CHUNK = 32
ROW_SLABS = D_MODEL // V7X_LANES
EXPERT_TILE = 256
ATT_UNROLL = 16
DMA_THREADS = 2
LOG2E = math.log2(math.e)
N_PAIRS = EXPERTS_PER_GROUP * (EXPERTS_PER_GROUP - 1) // 2
N_CLASSES = N_EXPERT_GROUPS * N_PAIRS
CLASS_ROWS = 32
NEG_BIG = -1e30

_PAIRS = [(a, b) for a in range(EXPERTS_PER_GROUP) for b in range(a + 1, EXPERTS_PER_GROUP)]


def _cparams(*sem):
    return pltpu.CompilerParams(dimension_semantics=sem, vmem_limit_bytes=V7X_VMEM_LIMIT_BYTES)


def _rms(x, g):
    return x * lax.rsqrt(jnp.mean(x * x, axis=-1, keepdims=True) + EPS) * g


def _load_row8(ref, n):
    return jnp.concatenate([ref[pl.ds(s, n, stride=ROW_SLABS), :] for s in range(ROW_SLABS)], axis=1)


def _store_row8(ref, val, n):
    for s in range(ROW_SLABS):
        ref[pl.ds(s, n, stride=ROW_SLABS), :] = val[:, s * V7X_LANES:(s + 1) * V7X_LANES]


def _bf16_bits(x):
    return lax.bitcast_convert_type(x.astype(jnp.bfloat16).astype(jnp.float32), jnp.uint32)


def _pack_pair(lo, hi):
    return (_bf16_bits(lo) >> 16) | _bf16_bits(hi)


def _unpack_lo(w):
    return lax.bitcast_convert_type(w << 16, jnp.float32)


def _unpack_hi(w):
    return lax.bitcast_convert_type(w & jnp.uint32(0xFFFF0000), jnp.float32)


def _a_layer_kernel(has_y, *refs):
    if has_y:
        x_ref, ys_ref, pos_ref, pos_next_ref, *refs = refs
    else:
        x_ref, *refs = refs
    (g_ref, win_ref, bin_ref, wdw_ref, bdw_ref, lng_ref, lnb_ref, wout_ref, bout_ref, fg_ref, wr_ref,
     o_ref, hp_ref, cls_ref, rank_ref, cnt_ref, ext_ref, sh_ref, conv_ref, wb_ref, carry_ref,
     *gather_scratch) = refs
    ts, d = x_ref.shape
    step = pl.program_id(0) * pl.num_programs(1) + pl.program_id(1)

    @pl.when(pl.program_id(1) == 0)
    def _():
        ext_ref[0:CONV_HALO, :] = jnp.zeros((CONV_HALO, d), jnp.float32)

    x = x_ref[...]
    if has_y:
        n_steps = pl.num_programs(0) * pl.num_programs(1)
        x = x + _gathered_rows(step, n_steps, ys_ref, pos_ref, pos_next_ref, *gather_scratch)
    h = _rms(x, g_ref[...])
    u = jnp.dot(h.astype(jnp.bfloat16), win_ref[...], preferred_element_type=jnp.float32) + bin_ref[...]
    ext_ref[CONV_HALO:CONV_HALO + ts, :] = u[:, :d] * jax.nn.sigmoid(u[:, d:])

    base = CONV_HALO - (CONV_KERNEL - 1)
    nsh = ts + CONV_HALO - V7X_SUBLANES
    for j in range(1, V7X_SUBLANES):
        sh_ref[j - 1] = ext_ref[pl.ds(j, nsh), :]

    @pl.when(step == 0)
    def _():
        for k in range(CONV_KERNEL):
            wb_ref[k] = jnp.broadcast_to(wdw_ref[k:k + 1, :], (V7X_SUBLANES, d))
        wb_ref[CONV_KERNEL] = jnp.broadcast_to(bdw_ref[...], (V7X_SUBLANES, d))

    groups = CONV_CHUNK // V7X_SUBLANES

    def chunk(c, carry):
        r = pl.multiple_of(c * CONV_CHUNK, CONV_CHUNK)
        accs = [wb_ref[CONV_KERNEL]] * groups
        for k in range(CONV_KERNEL):
            q, j = divmod(base + k, V7X_SUBLANES)
            w = wb_ref[k]
            for gi in range(groups):
                rows = pl.ds(r + (q + gi) * V7X_SUBLANES, V7X_SUBLANES)
                tap = ext_ref[rows, :] if j == 0 else sh_ref[j - 1, rows, :]
                accs[gi] = accs[gi] + w * tap
        for gi in range(groups):
            conv_ref[pl.ds(r + gi * V7X_SUBLANES, V7X_SUBLANES), :] = accs[gi]
        return carry

    lax.fori_loop(0, ts // CONV_CHUNK, chunk, 0)
    ext_ref[0:CONV_HALO, :] = ext_ref[ts:ts + CONV_HALO, :]

    c = conv_ref[...]
    mu = jnp.mean(c, axis=-1, keepdims=True)
    cc = c - mu
    var = jnp.mean(cc * cc, axis=-1, keepdims=True)
    v = cc * lax.rsqrt(var + EPS) * lng_ref[...] + lnb_ref[...]
    v = v * jax.nn.sigmoid(v)
    out = jnp.dot(v.astype(jnp.bfloat16), wout_ref[...], preferred_element_type=jnp.float32)
    x_new = x + out + bout_ref[...]
    o_ref[...] = x_new
    _route_rows(x_new, step == 0, fg_ref, wr_ref, hp_ref, cls_ref, rank_ref, cnt_ref, carry_ref)


def _row(v):
    return v.reshape(1, -1)


def _a_layer(x2, moe, router, batch, g, w_in, b_in, w_dw, b_dw, ln_g, ln_b, w_out, b_out):
    t, d = x2.shape
    ns = t // batch // SEQ_TILE
    step_of = lambda b, s: b * ns + s
    const = lambda shape: pl.BlockSpec(shape, lambda b, s: (0, 0))
    weight = lambda shape: pl.BlockSpec(shape, lambda b, s: (0, 0), pipeline_mode=pl.Buffered(1))
    tok = pl.BlockSpec((SEQ_TILE, d), lambda b, s: (step_of(b, s), 0))
    acts, act_specs, gather_scratch = [x2], [tok], []
    if moe is not None:
        g_args, g_specs, gather_scratch = _gather_operands(moe, batch * ns, step_of)
        acts += g_args
        act_specs += g_specs
    r_args, r_specs, r_shape, r_out_specs, r_scratch = _route_operands(t, d, router, step_of)
    x_new, *routed = pl.pallas_call(
        functools.partial(_a_layer_kernel, moe is not None),
        grid=(batch, ns),
        in_specs=act_specs + [
            const((1, d)), weight((d, 2 * d)), const((1, 2 * d)), const((CONV_KERNEL, d)), const((1, d)),
            const((1, d)), const((1, d)), weight((d, d)), const((1, d)),
        ] + r_specs,
        out_specs=[tok] + r_out_specs,
        out_shape=[jax.ShapeDtypeStruct((t, d), jnp.float32)] + r_shape,
        scratch_shapes=[pltpu.VMEM((CONV_HALO + SEQ_TILE, d), jnp.float32),
                        pltpu.VMEM((V7X_SUBLANES - 1, CONV_HALO + SEQ_TILE - V7X_SUBLANES, d), jnp.float32),
                        pltpu.VMEM((SEQ_TILE, d), jnp.float32),
                        pltpu.VMEM((CONV_KERNEL + 1, V7X_SUBLANES, d), jnp.float32)] + r_scratch + gather_scratch,
        compiler_params=_cparams("arbitrary", "arbitrary"),
        name="a_layer",
    )(*acts, _row(g), w_in.astype(jnp.bfloat16), _row(b_in), w_dw, _row(b_dw), _row(ln_g), _row(ln_b),
      w_out.astype(jnp.bfloat16), _row(b_out), *r_args)
    return x_new, routed


def _route_rows(x, first_step, g_ref, wr_ref, hp_ref, cls_ref, rank_ref, cnt_ref, carry_ref):
    tr = x.shape[0]

    @pl.when(first_step)
    def _():
        carry_ref[...] = jnp.zeros_like(carry_ref)

    h = _rms(x, g_ref[...])
    lt = lax.dot_general(wr_ref[...], h.astype(jnp.bfloat16), (((1,), (1,)), ((), ())),
                         preferred_element_type=jnp.float32)
    lg = [lt[i:i + 1, :] for i in range(N_EXPERT_GROUPS)]
    gmax = functools.reduce(jnp.maximum, lg)
    g_idx = jnp.full_like(gmax, N_EXPERT_GROUPS - 1, dtype=jnp.int32)
    for i in reversed(range(N_EXPERT_GROUPS - 1)):
        g_idx = jnp.where(lg[i] == gmax, i, g_idx)
    p_top = 1.0 / functools.reduce(jnp.add, [jnp.exp(v - gmax) for v in lg])
    sel = []
    for e in range(EXPERTS_PER_GROUP):
        v = lt[N_EXPERT_GROUPS + e:N_EXPERT_GROUPS + e + 1, :]
        for g in range(1, N_EXPERT_GROUPS):
            r = N_EXPERT_GROUPS + g * EXPERTS_PER_GROUP + e
            v = jnp.where(g_idx == g, lt[r:r + 1, :], v)
        sel.append(v)
    v1 = functools.reduce(jnp.maximum, sel)
    i1 = jnp.full_like(g_idx, EXPERTS_PER_GROUP - 1)
    for e in reversed(range(EXPERTS_PER_GROUP - 1)):
        i1 = jnp.where(sel[e] == v1, e, i1)
    rest = [jnp.where(i1 == e, -jnp.inf, sel[e]) for e in range(EXPERTS_PER_GROUP)]
    v2 = functools.reduce(jnp.maximum, rest)
    i2 = jnp.full_like(g_idx, EXPERTS_PER_GROUP - 1)
    for e in reversed(range(EXPERTS_PER_GROUP - 1)):
        i2 = jnp.where((rest[e] == v2) & (i1 != e), e, i2)
    i2 = jnp.where((i2 == i1), jnp.where(i1 == EXPERTS_PER_GROUP - 1, EXPERTS_PER_GROUP - 2, i2), i2)
    t2 = jnp.exp(v2 - v1)
    w1 = p_top / (1.0 + t2)
    w2 = p_top * t2 / (1.0 + t2)
    first_lo = i1 < i2
    e_lo = jnp.where(first_lo, i1, i2)
    e_hi = jnp.where(first_lo, i2, i1)
    w_lo = jnp.where(first_lo, w1, w2)
    w_hi = jnp.where(first_lo, w2, w1)
    pair_base = jnp.where(e_lo == 0, 0, jnp.where(e_lo == 1, EXPERTS_PER_GROUP - 1, 2 * EXPERTS_PER_GROUP - 3))
    cls = g_idx * N_PAIRS + pair_base + (e_hi - e_lo - 1)

    onehot = (lax.broadcasted_iota(jnp.int32, (CLASS_ROWS, tr), 0) == cls)
    before = (lax.broadcasted_iota(jnp.int32, (tr, tr), 0) < lax.broadcasted_iota(jnp.int32, (tr, tr), 1))
    cum = jnp.dot(onehot.astype(jnp.bfloat16), before.astype(jnp.bfloat16), preferred_element_type=jnp.float32)
    oh = onehot.astype(jnp.float32)
    rank = jnp.sum(oh * (cum + carry_ref[...]), axis=0, keepdims=True)
    carry_ref[...] = carry_ref[...] + jnp.sum(oh, axis=1, keepdims=True)
    cls_ref[0] = cls
    rank_ref[0] = rank.astype(jnp.int32)
    cnt_ref[...] = jnp.broadcast_to(carry_ref[...], cnt_ref.shape)

    half = D_MODEL // 2
    word = _pack_pair(h[:, :half], h[:, half:])
    n_word = half // V7X_LANES
    for s in range(n_word):
        hp_ref[pl.ds(s, tr, stride=ROW_SLABS), :] = word[:, s * V7X_LANES:(s + 1) * V7X_LANES]
    wcols = jnp.concatenate([w_lo, w_hi, jnp.zeros((V7X_LANES - 2, tr), jnp.float32)], axis=0).T
    hp_ref[pl.ds(n_word, tr, stride=ROW_SLABS), :] = lax.bitcast_convert_type(wcols, jnp.uint32)
    for s in range(n_word + 1, ROW_SLABS):
        hp_ref[pl.ds(s, tr, stride=ROW_SLABS), :] = jnp.zeros((tr, V7X_LANES), jnp.uint32)


def _route_operands(t, d, router, step_of):
    g, r_grp, r_exp = router
    nt = t // SEQ_TILE
    wr = jnp.concatenate([r_grp, jnp.transpose(r_exp, (1, 0, 2)).reshape(d, N_EXPERTS)], axis=1)
    wr = jnp.pad(wr, ((0, 0), (0, CLASS_ROWS - wr.shape[1]))).T.astype(jnp.bfloat16)
    const = lambda shape: pl.BlockSpec(shape, lambda *g_: (0, 0))
    vec = lambda dt: jax.ShapeDtypeStruct((nt, 1, SEQ_TILE), dt)
    vec_spec = pl.BlockSpec((1, 1, SEQ_TILE), lambda *g_: (step_of(*g_), 0, 0))
    return ([_row(g), wr], [const((1, d)), const((CLASS_ROWS, d))],
            [jax.ShapeDtypeStruct((t * ROW_SLABS, V7X_LANES), jnp.uint32), vec(jnp.int32), vec(jnp.int32),
             jax.ShapeDtypeStruct((CLASS_ROWS, V7X_LANES), jnp.float32)],
            [pl.BlockSpec((SEQ_TILE * ROW_SLABS, V7X_LANES), lambda *g_: (step_of(*g_), 0)), vec_spec, vec_spec,
             const((CLASS_ROWS, V7X_LANES))],
            [pltpu.VMEM((CLASS_ROWS, 1), jnp.float32)])


def _row_slice(ref, row):
    return ref.at[pl.ds(pl.multiple_of(row * ROW_SLABS, ROW_SLABS), ROW_SLABS)]


def _scatter_kernel(pos_ref, pad_ref, src_ref, dst_ref, zero_ref, sem, zero_sem):
    n = pos_ref.shape[2]
    tile_rows = EXPERT_TILE * ROW_SLABS

    def zero_copy(row):
        return pltpu.make_async_copy(zero_ref.at[pl.ds(0, ROW_SLABS)], _row_slice(dst_ref, row), zero_sem)

    def zero_tile_copy(j):
        return pltpu.make_async_copy(zero_ref, dst_ref.at[pl.ds(pl.multiple_of(j * tile_rows, tile_rows), tile_rows)],
                                     zero_sem)

    def for_each_pad_row(fn):
        for c in range(N_CLASSES):
            first = pad_ref[0, 0, c]
            lax.fori_loop(0, pad_ref[0, 1, c], lambda k, carry: fn(first + k) or carry, 0)

    def for_each_unused_tile(fn):
        lax.fori_loop(pad_ref[0, 0, N_CLASSES], dst_ref.shape[0] // tile_rows, lambda j, carry: fn(j) or carry, 0)

    @pl.when(pl.program_id(0) == 0)
    def _():
        zero_ref[...] = jnp.zeros_like(zero_ref)
        for_each_pad_row(lambda row: zero_copy(row).start())
        for_each_unused_tile(lambda j: zero_tile_copy(j).start())

    def issue(i, carry):
        for u in range(DMA_THREADS):
            k = i * DMA_THREADS + u
            pltpu.make_async_copy(_row_slice(src_ref, k), _row_slice(dst_ref, pos_ref[0, 0, k]),
                                  sem).start(priority=u)
        return carry

    lax.fori_loop(0, n // DMA_THREADS, issue, 0)
    pltpu.make_async_copy(src_ref, dst_ref.at[pl.ds(0, n * ROW_SLABS)], sem).wait()

    @pl.when(pl.program_id(0) == 0)
    def _():
        for_each_pad_row(lambda row: zero_copy(row).wait())
        for_each_unused_tile(lambda j: zero_tile_copy(j).wait())


def _row_scatter(src, pos3, pad, n_dst_rows):
    nt, _, n = pos3.shape
    return pl.pallas_call(
        _scatter_kernel,
        grid=(nt,),
        in_specs=[pl.BlockSpec((1, 1, n), lambda i: (i, 0, 0), memory_space=pltpu.SMEM),
                  pl.BlockSpec(pad.shape, lambda i: (0, 0, 0), memory_space=pltpu.SMEM),
                  pl.BlockSpec((n * ROW_SLABS, V7X_LANES), lambda i: (i, 0))],
        out_specs=pl.BlockSpec(memory_space=pl.ANY),
        out_shape=jax.ShapeDtypeStruct((n_dst_rows * ROW_SLABS, V7X_LANES), src.dtype),
        scratch_shapes=[pltpu.VMEM((EXPERT_TILE * ROW_SLABS, V7X_LANES), src.dtype), pltpu.SemaphoreType.DMA,
                        pltpu.SemaphoreType.DMA],
        compiler_params=_cparams("arbitrary"),
        name="row_scatter",
    )(pos3, pad, src)


def _gather_copy(ys_ref, pos_ref, buf_ref, sem, slot, k):
    return pltpu.make_async_copy(_row_slice(ys_ref, pos_ref[0, 0, k]), _row_slice(buf_ref.at[slot], k), sem.at[slot])


def _gathered_rows(step, n_steps, ys_ref, pos_ref, pos_next_ref, buf_ref, sem):
    slot = step % 2
    n_tok = pos_ref.shape[2]

    @pl.when(step == 0)
    def _():
        def issue(k, carry):
            _gather_copy(ys_ref, pos_ref, buf_ref, sem, 0, k).start()
            return carry

        lax.fori_loop(0, n_tok, issue, 0)

    def wait(s):
        pltpu.make_async_copy(ys_ref.at[pl.ds(0, n_tok * ROW_SLABS)], buf_ref.at[s], sem.at[s]).wait()

    wait(slot)
    y = _load_row8(buf_ref.at[slot], n_tok)
    for k in range(n_tok):
        _gather_copy(ys_ref, pos_next_ref, buf_ref, sem, 1 - slot, k).start(priority=k % DMA_THREADS)

    @pl.when(step == n_steps - 1)
    def _():
        wait(1 - slot)

    return y


def _gather_operands(moe, n_steps, step_of):
    ys, pos3 = moe
    n = pos3.shape[2]
    smem = lambda m: pl.BlockSpec((1, 1, n), m, memory_space=pltpu.SMEM)
    cur = lambda *g: (step_of(*g), 0, 0)
    nxt = lambda *g: (jnp.minimum(step_of(*g) + 1, n_steps - 1), 0, 0)
    return ([ys, pos3, pos3], [pl.BlockSpec(memory_space=pl.ANY), smem(cur), smem(nxt)],
            [pltpu.VMEM((2, n * ROW_SLABS, V7X_LANES), jnp.float32), pltpu.SemaphoreType.DMA((2,))])


def _expert_kernel(lo_ref, hi_ref, nt_ref, hs_ref, wg_lo, wu_lo, wd_lo, wg_hi, wu_hi, wd_hi, y_ref):
    tm = EXPERT_TILE
    j = pl.program_id(0)

    @pl.when(j < nt_ref[0])
    def _():
        n_word = D_MODEL // 2 // V7X_LANES
        words = [hs_ref[pl.ds(s, tm, stride=ROW_SLABS), :] for s in range(n_word)]
        h = jnp.concatenate([_unpack_lo(w) for w in words] + [_unpack_hi(w) for w in words], axis=1)
        h = h.astype(jnp.bfloat16)
        gates = lax.bitcast_convert_type(hs_ref[pl.ds(n_word, tm, stride=ROW_SLABS), :], jnp.float32)
        y = None
        for col, (wg, wu, wd) in enumerate(((wg_lo, wu_lo, wd_lo), (wg_hi, wu_hi, wd_hi))):
            a = jnp.dot(h, wg[0], preferred_element_type=jnp.float32)
            u = jnp.dot(h, wu[0], preferred_element_type=jnp.float32)
            hid = a * jax.nn.sigmoid(a) * u * gates[:, col:col + 1]
            part = jnp.dot(hid.astype(jnp.bfloat16), wd[0], preferred_element_type=jnp.float32)
            y = part if y is None else y + part
        _store_row8(y_ref, y, tm)

    @pl.when(j >= nt_ref[0])
    def _():
        y_ref[...] = jnp.zeros_like(y_ref)


def _moe_experts(hs, tile_lo, tile_hi, n_tiles, w_gate, w_up, w_down):
    n_rows = hs.shape[0] // ROW_SLABS
    nt_max = n_rows // EXPERT_TILE
    d, f = w_gate.shape[1], w_gate.shape[2]
    act = lambda j, lo, hi, nt: (jnp.maximum(jnp.minimum(j, nt[0] - 1), 0), 0)
    w_lo = lambda j, lo, hi, nt: (lo[j], 0, 0)
    w_hi = lambda j, lo, hi, nt: (hi[j], 0, 0)
    up_spec = lambda m: pl.BlockSpec((1, d, f), m)
    dn_spec = lambda m: pl.BlockSpec((1, f, d), m)
    blk = (EXPERT_TILE * ROW_SLABS, V7X_LANES)
    wg, wu, wd = w_gate, w_up, w_down
    return pl.pallas_call(
        _expert_kernel,
        grid_spec=pltpu.PrefetchScalarGridSpec(
            num_scalar_prefetch=3,
            grid=(nt_max,),
            in_specs=[pl.BlockSpec(blk, act), up_spec(w_lo), up_spec(w_lo), dn_spec(w_lo),
                      up_spec(w_hi), up_spec(w_hi), dn_spec(w_hi)],
            out_specs=pl.BlockSpec(blk, lambda j, lo, hi, nt: (j, 0)),
        ),
        out_shape=jax.ShapeDtypeStruct((n_rows * ROW_SLABS, V7X_LANES), jnp.float32),
        compiler_params=_cparams("arbitrary"),
        name="moe_experts",
    )(tile_lo, tile_hi, n_tiles, hs, wg, wu, wd, wg, wu, wd)


def _moe(routed, layer, w_gate, w_up, w_down):
    hp, cls, rank, cnt = routed
    t = hp.shape[0] // ROW_SLABS
    counts = cnt[:N_CLASSES, 0].astype(jnp.int32)
    tiles_per = (counts + EXPERT_TILE - 1) // EXPERT_TILE
    tile_end = jnp.cumsum(tiles_per)
    base = (tile_end - tiles_per) * EXPERT_TILE
    cls = cls.reshape(t)
    pos = jnp.sum(jnp.where(cls[:, None] == jnp.arange(N_CLASSES)[None, :], base[None, :], 0), axis=1)
    pos = (pos + rank.reshape(t)).astype(jnp.int32)
    nt_max = t // EXPERT_TILE + N_CLASSES
    tile_cls = jnp.sum(jnp.arange(nt_max)[:, None] >= tile_end[None, :], axis=1)
    tile_cls = jnp.minimum(tile_cls, N_CLASSES - 1)
    lo_tab = jnp.array([p[0] for p in _PAIRS], jnp.int32)
    hi_tab = jnp.array([p[1] for p in _PAIRS], jnp.int32)
    tile_g = tile_cls // N_PAIRS + layer * N_EXPERT_GROUPS
    tile_lo = (tile_g * EXPERTS_PER_GROUP + lo_tab[tile_cls % N_PAIRS]).astype(jnp.int32)
    tile_hi = (tile_g * EXPERTS_PER_GROUP + hi_tab[tile_cls % N_PAIRS]).astype(jnp.int32)
    n_tiles = tile_end[-1:].astype(jnp.int32)
    n_sorted = nt_max * EXPERT_TILE
    pos3 = pos.reshape(t // SEQ_TILE, 1, SEQ_TILE)
    pad = jnp.stack([base + counts, tiles_per * EXPERT_TILE - counts]).astype(jnp.int32)
    pad = jnp.pad(pad, ((0, 0), (0, CLASS_ROWS - N_CLASSES)))
    pad = pad.at[0, N_CLASSES].set(n_tiles[0])[None]
    hs = _row_scatter(hp, pos3, pad, n_sorted)
    ys = _moe_experts(hs, tile_lo, tile_hi, n_tiles, w_gate, w_up, w_down)
    return ys, pos3


def _rope_tables(seq):
    pos = jnp.arange(seq, dtype=jnp.float32)
    inv_freq = ROPE_THETA ** (-jnp.arange(0, HEAD_DIM, 2, dtype=jnp.float32) / HEAD_DIM)
    ang = pos[:, None] * inv_freq[None, :]
    reps = V7X_LANES // (HEAD_DIM // 2)
    cos = jnp.tile(jnp.cos(ang), (1, reps))
    sign = jnp.tile(jnp.concatenate([-jnp.ones(HEAD_DIM // 2), jnp.ones(HEAD_DIM // 2)]), V7X_LANES // HEAD_DIM)
    sin = jnp.tile(jnp.sin(ang), (1, reps)) * sign[None, :]
    return cos, sin


def _rope_slab(v, cos, sin, lane):
    half = HEAD_DIM // 2
    partner = jnp.where((lane % HEAD_DIM) < half, pltpu.roll(v, V7X_LANES - half, axis=1), pltpu.roll(v, half, axis=1))
    return v * cos + partner * sin


def _proj_kernel(has_y, specs, *refs):
    n = len(specs)
    if has_y:
        x_ref, ys_ref, pos_ref, pos_next_ref, *refs = refs
    else:
        x_ref, *refs = refs
    cos_ref, sin_ref, *refs = refs
    g_refs, w_refs, refs = refs[:n], refs[n:2 * n], refs[2 * n:]
    if has_y:
        xo_ref, *refs = refs
    o_refs, gather_scratch = refs[:n], refs[n:]
    ts = x_ref.shape[0]
    x = x_ref[...]
    if has_y:
        x = x + _gathered_rows(pl.program_id(0), pl.num_programs(0), ys_ref, pos_ref, pos_next_ref, *gather_scratch)
        xo_ref[...] = x
    inv = lax.rsqrt(jnp.mean(x * x, axis=-1, keepdims=True) + EPS)
    xn = x * inv
    cos, sin = cos_ref[...], sin_ref[...]
    lane = lax.broadcasted_iota(jnp.int32, (ts, V7X_LANES), 1)
    for i, (use_rope, scale) in enumerate(specs):
        h = (xn * g_refs[i][...]).astype(jnp.bfloat16)
        for grp in range(N_DGROUPS):
            acc = jnp.dot(h, w_refs[i][:, grp * D_MODEL:(grp + 1) * D_MODEL], preferred_element_type=jnp.float32)
            slabs = []
            for s in range(ROW_SLABS):
                v = acc[:, s * V7X_LANES:(s + 1) * V7X_LANES]
                if use_rope:
                    v = _rope_slab(v, cos, sin, lane)
                if scale != 1.0:
                    v = v * scale
                slabs.append(v)
            for q in range(ROW_SLABS // 2):
                o_refs[i][grp, :, q * V7X_LANES:(q + 1) * V7X_LANES] = _pack_pair(slabs[2 * q], slabs[2 * q + 1])


def _qkv_proj(x2, moe, seq, projections):
    t, d = x2.shape
    ns = seq // SEQ_TILE
    n = len(projections)
    cos, sin = _rope_tables(seq)
    tok = pl.BlockSpec((SEQ_TILE, d), lambda i: (i, 0))
    rope_spec = pl.BlockSpec((SEQ_TILE, V7X_LANES), lambda i: (i % ns, 0))
    acts, act_specs, gather_scratch = [x2], [tok], []
    if moe is not None:
        g_args, g_specs, gather_scratch = _gather_operands(moe, t // SEQ_TILE, lambda i: i)
        acts += g_args
        act_specs += g_specs
    packed = jax.ShapeDtypeStruct((N_DGROUPS, t, d // 2), jnp.uint32)
    packed_spec = pl.BlockSpec((N_DGROUPS, SEQ_TILE, d // 2), lambda i: (0, i, 0))
    out_shape, out_specs = [packed] * n, [packed_spec] * n
    if moe is not None:
        out_shape = [jax.ShapeDtypeStruct((t, d), jnp.float32)] + out_shape
        out_specs = [tok] + out_specs
    specs = tuple((p[3], p[4]) for p in projections)
    w_spec = lambda col: pl.BlockSpec((d, N_DGROUPS * d), lambda i: (0, col), pipeline_mode=pl.Buffered(1))
    bf16_of = {}
    for p in projections:
        bf16_of.setdefault(id(p[1]), p[1].astype(jnp.bfloat16))
    return pl.pallas_call(
        functools.partial(_proj_kernel, moe is not None, specs),
        grid=(t // SEQ_TILE,),
        in_specs=act_specs + [rope_spec, rope_spec]
                 + [pl.BlockSpec((1, d), lambda i: (0, 0))] * n
                 + [w_spec(p[2]) for p in projections],
        out_specs=out_specs,
        out_shape=out_shape,
        scratch_shapes=gather_scratch,
        compiler_params=_cparams("arbitrary"),
        name="qkv_proj",
    )(*acts, cos, sin, *[_row(p[0]) for p in projections], *[bf16_of[id(p[1])] for p in projections])


def _attn_unit(pair, qw, kw, vw, bias, lane):
    unpack = _unpack_lo if pair == 0 else _unpack_hi
    q = unpack(qw).astype(jnp.bfloat16)
    k = unpack(kw).astype(jnp.bfloat16)
    v = unpack(vw).astype(jnp.bfloat16)
    first = lane < HEAD_DIM
    zero = jnp.zeros_like(q)
    qs = jnp.concatenate([jnp.where(first, q, zero), jnp.where(first, zero, q)], axis=0)
    s = lax.dot_general(qs, k, (((1,), (1,)), ((), ())), preferred_element_type=jnp.float32) + bias
    m = jnp.max(s, axis=1, keepdims=True)
    p = jnp.exp2(s - m)
    nq = ATT_BLOCK
    shape = (nq, V7X_LANES)
    pb = p.astype(jnp.bfloat16)
    if k.shape[0] > nq:
        pv = jnp.dot(pb, jnp.concatenate([v, jnp.ones_like(v)], axis=1), preferred_element_type=jnp.float32)
        l2 = jnp.where(first, pv[:nq, V7X_LANES:], pv[nq:, V7X_LANES:])
    else:
        pv = jnp.dot(pb, v, preferred_element_type=jnp.float32)
        l = jnp.sum(p, axis=1, keepdims=True)
        l2 = jnp.where(first, jnp.broadcast_to(l[:nq], shape), jnp.broadcast_to(l[nq:], shape))
    o = jnp.where(first, pv[:nq, :V7X_LANES], pv[nq:, :V7X_LANES])
    m2 = jnp.where(first, jnp.broadcast_to(m[:nq], shape), jnp.broadcast_to(m[nq:], shape))
    return o, m2, l2


def _attn_kernel(q_ref, k_ref, v_ref, o_ref, acc_ref, m_ref, l_ref):
    seq = q_ref.shape[1]
    nq = ATT_BLOCK
    lane = lax.broadcasted_iota(jnp.int32, (nq, V7X_LANES), 1)
    qi = lax.broadcasted_iota(jnp.int32, (2 * nq, 2 * nq), 0) % nq
    kj = lax.broadcasted_iota(jnp.int32, (2 * nq, 2 * nq), 1)
    band_prev = (kj < nq) & (kj >= qi)
    band_cur = (kj >= nq) & (kj - nq <= qi)
    bias_full = jnp.where(band_prev | band_cur, 0.0, NEG_BIG)
    bias_cur = bias_full[:, nq:]
    prev_cols = jnp.where(kj < nq, NEG_BIG, 0.0)

    order = sorted(range(N_DGROUPS), key=lambda g_: -DILATED_GROUPS[g_][1])
    for grp in order:
        dil = DILATED_GROUPS[grp][1]
        nblk = seq // dil // nq
        assert nblk % ATT_UNROLL == 0 or ATT_UNROLL % nblk == 0

        def rows(start):
            return pl.ds(pl.multiple_of(start, nq), nq) if dil == 1 else pl.ds(start, nq, stride=dil)

        def unit(it, u):
            idx = it * ATT_UNROLL + u
            n = u % nblk if ATT_UNROLL % nblk == 0 else idx % nblk
            r = idx // nblk
            start = n * nq * dil + r
            qw = q_ref[grp, rows(start), :]
            kw, vw = k_ref[grp, rows(start), :], v_ref[grp, rows(start), :]
            if isinstance(n, int) and n == 0:
                bias = bias_cur
            else:
                prev = jnp.maximum(n - 1, 0) * nq * dil + r
                kw = jnp.concatenate([k_ref[grp, rows(prev), :], kw], axis=0)
                vw = jnp.concatenate([v_ref[grp, rows(prev), :], vw], axis=0)
                bias = bias_full
                if not isinstance(n, int) and u == 0:
                    bias = bias + jnp.where(n > 0, 0.0, 1.0) * prev_cols
            for pair in range(2):
                o, m2, l2 = _attn_unit(pair, qw, kw, vw, bias, lane)
                if grp == order[0]:
                    acc_ref[pair, rows(start), :] = o
                    m_ref[pair, rows(start), :] = m2
                    l_ref[pair, rows(start), :] = l2
                else:
                    m_old = m_ref[pair, rows(start), :]
                    m_new = jnp.maximum(m_old, m2)
                    a_old, a_new = jnp.exp2(m_old - m_new), jnp.exp2(m2 - m_new)
                    acc_ref[pair, rows(start), :] = acc_ref[pair, rows(start), :] * a_old + o * a_new
                    l_ref[pair, rows(start), :] = l_ref[pair, rows(start), :] * a_old + l2 * a_new
                    m_ref[pair, rows(start), :] = m_new

        def units(it, carry):
            for u in range(ATT_UNROLL):
                unit(it, u)
            return carry

        lax.fori_loop(0, seq // nq // ATT_UNROLL, units, 0)

    for pair in range(2):
        o_ref[:, pair * V7X_LANES:(pair + 1) * V7X_LANES] = (acc_ref[pair] / l_ref[pair]).astype(o_ref.dtype)


def _dilated_attention(qp, kp, vp, batch):
    t = qp.shape[1]
    seq = t // batch
    nquad = qp.shape[2] // V7X_LANES
    spec = pl.BlockSpec((N_DGROUPS, seq, V7X_LANES), lambda b, c: (0, b, c))
    return pl.pallas_call(
        _attn_kernel,
        grid=(batch, nquad),
        in_specs=[spec, spec, spec],
        out_specs=pl.BlockSpec((seq, 2 * V7X_LANES), lambda b, c: (b, c)),
        out_shape=jax.ShapeDtypeStruct((t, D_MODEL), jnp.bfloat16),
        scratch_shapes=[pltpu.VMEM((2, seq, V7X_LANES), jnp.float32)] * 3,
        compiler_params=_cparams("arbitrary", "arbitrary"),
        name="dilated_attention",
    )(qp, kp, vp)


def _attn_out_kernel(x_ref, c_ref, w_ref, fg_ref, wr_ref, o_ref, hp_ref, cls_ref, rank_ref, cnt_ref, carry_ref):
    x_new = x_ref[...] + jnp.dot(c_ref[...], w_ref[...], preferred_element_type=jnp.float32)
    o_ref[...] = x_new
    _route_rows(x_new, pl.program_id(0) == 0, fg_ref, wr_ref, hp_ref, cls_ref, rank_ref, cnt_ref, carry_ref)


def _attn_out(x2, comb, w_o, router):
    t, d = x2.shape
    tok = pl.BlockSpec((SEQ_TILE, d), lambda i: (i, 0))
    r_args, r_specs, r_shape, r_out_specs, r_scratch = _route_operands(t, d, router, lambda i: i)
    x_new, *routed = pl.pallas_call(
        _attn_out_kernel,
        grid=(t // SEQ_TILE,),
        in_specs=[tok, tok, pl.BlockSpec((d, d), lambda i: (0, 0))] + r_specs,
        out_specs=[tok] + r_out_specs,
        out_shape=[jax.ShapeDtypeStruct((t, d), jnp.float32)] + r_shape,
        scratch_shapes=r_scratch,
        compiler_params=_cparams("arbitrary"),
        name="attn_out",
    )(x2, comb, w_o.astype(jnp.bfloat16), *r_args)
    return x_new, routed


def _final_kernel(x_ref, ys_ref, pos_ref, pos_next_ref, g_ref, o_ref, *gather_scratch):
    y = _gathered_rows(pl.program_id(0), pl.num_programs(0), ys_ref, pos_ref, pos_next_ref, *gather_scratch)
    o_ref[...] = _rms(x_ref[...] + y, g_ref[...])


def _final_norm(x2, moe, g):
    t, d = x2.shape
    tok = pl.BlockSpec((SEQ_TILE, d), lambda i: (i, 0))
    g_args, g_specs, gather_scratch = _gather_operands(moe, t // SEQ_TILE, lambda i: i)
    return pl.pallas_call(
        _final_kernel,
        grid=(t // SEQ_TILE,),
        in_specs=[tok] + g_specs + [pl.BlockSpec((1, d), lambda i: (0, 0))],
        out_specs=tok,
        out_shape=jax.ShapeDtypeStruct((t, d), jnp.float32),
        scratch_shapes=gather_scratch,
        compiler_params=_cparams("arbitrary"),
        name="final_norm",
    )(x2, *g_args, _row(g))


def kernel(x, a_norm, a_w_in, a_b_in, a_w_dw, a_b_dw, a_ln_g, a_ln_b, a_w_out, a_b_out, kv_norm, w_kv, b_norm,
           b_w_q, b_w_o, ffn_norm, router_group, router_expert, w_gate, w_up, w_down, final_norm):
    batch, seq, d = x.shape
    n_a = a_norm.shape[0]
    depth = ffn_norm.shape[0]
    q_scale = HEAD_DIM ** -0.5 * LOG2E
    x2 = x.reshape(batch * seq, d)
    expert_w = [w.reshape((depth * N_EXPERTS,) + w.shape[2:]).astype(jnp.bfloat16) for w in (w_gate, w_up, w_down)]
    moe = None
    kp = vp = None
    for l in range(depth):
        router = (ffn_norm[l], router_group[l], router_expert[l])
        if l < n_a:
            x2, routed = _a_layer(x2, moe, router, batch, a_norm[l], a_w_in[l], a_b_in[l], a_w_dw[l], a_b_dw[l],
                                  a_ln_g[l], a_ln_b[l], a_w_out[l], a_b_out[l])
        else:
            j = l - n_a
            projections = [(b_norm[j], b_w_q[j], 0, True, q_scale)]
            if kp is None:
                projections += [(kv_norm, w_kv, 0, True, 1.0), (kv_norm, w_kv, 1, False, 1.0)]
            outs = _qkv_proj(x2, moe, seq, projections)
            if moe is not None:
                x2, *outs = outs
            if kp is None:
                qp, kp, vp = outs
            else:
                qp, = outs
            comb = _dilated_attention(qp, kp, vp, batch)
            x2, routed = _attn_out(x2, comb, b_w_o[j], router)
        moe = _moe(routed, l, *expert_w)
    return _final_norm(x2, moe, final_norm).reshape(batch, seq, d)
```
